```python
import math
import jax, jax.numpy as jnp
from jax import lax
import numpy as np

D_MODEL = 1024
BATCH = 8
SEQ = 4096
DEPTH = 1

D_MIX = D_MODEL
D_SSM = D_MIX // 2
SSM_GROUP = 16
N_SSM_GROUPS = D_SSM // SSM_GROUP
SSM_STATE = 64
D_ATTN = D_MIX - D_SSM
HEAD_DIM = 64
N_HEADS = D_ATTN // HEAD_DIM
N_KV_HEADS = 2
KV_REP = N_HEADS // N_KV_HEADS
D_KV = N_KV_HEADS * HEAD_DIM
D_IN = D_SSM + D_ATTN + 2 * D_KV
WINDOW = 128
BLOCK = WINDOW
ROPE_THETA = 500000.0
ROPE_DIM = HEAD_DIM // 4
D_FF = ((8 * D_MODEL // 3 + 255) // 256) * 256
RES_HALF = 0.5
EPS = 1e-6
NEG_INF = -1e30
DT_MIN = 1e-3
DT_MAX = 1e-1

kernel_name = "hymba_s5_swa_sink_macaron"

F32 = jnp.float32


def rms_norm(x, g):
    xf = x.astype(F32)
    y = xf * lax.rsqrt(jnp.mean(xf * xf, axis=-1, keepdims=True) + EPS)
    return (y * g.astype(F32)).astype(x.dtype)


def swiglu(h, w_gate, w_up, w_down):
    return (jax.nn.silu(h @ w_gate) * (h @ w_up)) @ w_down


def rope_tables(L):
    half = ROPE_DIM // 2
    inv_freq = ROPE_THETA ** (-jnp.arange(half, dtype=F32) * 2.0 / ROPE_DIM)
    ang = jnp.arange(L, dtype=F32)[:, None] * inv_freq[None, :]
    return jnp.cos(ang), jnp.sin(ang)


def partial_rope(t, cos, sin):
    half = ROPE_DIM // 2
    c = cos[None, :, None, :]
    s = sin[None, :, None, :]
    t1 = t[..., :half].astype(F32)
    t2 = t[..., half:ROPE_DIM].astype(F32)
    rot = jnp.concatenate([t1 * c - t2 * s, t2 * c + t1 * s], axis=-1).astype(t.dtype)
    return jnp.concatenate([rot, t[..., ROPE_DIM:]], axis=-1)


def s5_mixer(u, A_re, A_im, log_dt, B_re, B_im, C_re, C_im, D, w_glu, b_glu):
    Bsz, L, _ = u.shape
    uf = u.astype(F32).reshape(Bsz, L, N_SSM_GROUPS, SSM_GROUP)
    lam = lax.complex(A_re.astype(F32), A_im.astype(F32))
    dt = jnp.exp(log_dt.astype(F32))[:, None]
    lam_bar = jnp.exp(lam * dt)
    zoh = (lam_bar - 1.0) / lam
    B_bar = zoh[..., None] * lax.complex(B_re.astype(F32), B_im.astype(F32))
    bu = lax.complex(jnp.einsum('blgc,gpc->blgp', uf, B_bar.real),
                     jnp.einsum('blgc,gpc->blgp', uf, B_bar.imag))
    a = jnp.broadcast_to(lam_bar, (1, L) + lam_bar.shape)

    def combine(left, right):
        a_l, b_l = left
        a_r, b_r = right
        return a_l * a_r, a_r * b_l + b_r

    _, states = lax.associative_scan(combine, (a, bu), axis=1)
    y = (jnp.einsum('blgp,gcp->blgc', states.real, C_re.astype(F32))
         - jnp.einsum('blgp,gcp->blgc', states.imag, C_im.astype(F32)))
    y = y + D.astype(F32).reshape(N_SSM_GROUPS, SSM_GROUP) * uf
    y = jax.nn.gelu(y.reshape(Bsz, L, D_SSM))
    y = y * jax.nn.sigmoid(y @ w_glu.astype(F32) + b_glu.astype(F32))
    return y.astype(u.dtype)


def swa_sink_attention(q, k, v, sinks):
    Bsz, L = q.shape[:2]
    nb = L // BLOCK
    qb = q.reshape(Bsz, nb, BLOCK, N_KV_HEADS, KV_REP, HEAD_DIM)

    def band(t):
        tb = t.reshape(Bsz, nb, BLOCK, N_KV_HEADS, HEAD_DIM)
        prev = jnp.pad(tb[:, :-1], ((0, 0), (1, 0), (0, 0), (0, 0), (0, 0)))
        return jnp.concatenate([prev, tb], axis=2)

    kw, vw = band(k), band(v)
    scale = 1.0 / math.sqrt(HEAD_DIM)
    scores = jnp.einsum('bnqkrd,bnskd->bnkrqs', qb, kw).astype(F32) * scale
    qi = jnp.arange(BLOCK)[:, None]
    sj = jnp.arange(2 * BLOCK)[None, :]
    diff = qi + BLOCK - sj
    in_band = (diff >= 0) & (diff < WINDOW)
    blk = jnp.arange(nb)[:, None, None]
    k_valid = (blk * BLOCK - BLOCK + sj[None]) >= 0
    mask = in_band[None] & k_valid
    scores = jnp.where(mask[None, :, None, None], scores, NEG_INF)
    sink = sinks.astype(F32).reshape(N_KV_HEADS, KV_REP)[None, None, :, :, None, None]
    sink = jnp.broadcast_to(sink, scores.shape[:-1] + (1,))
    probs = jax.nn.softmax(jnp.concatenate([scores, sink], axis=-1), axis=-1)[..., :-1]
    out = jnp.einsum('bnkrqs,bnskd->bnqkrd', probs.astype(v.dtype), vw)
    return out.reshape(Bsz, L, N_HEADS * HEAD_DIM)


def setup_inputs(seed: int = 0) -> dict:
    key = jax.random.key(seed)
    ks = jax.random.split(key, 32)
    nrm = lambda k, shape, s: jax.random.normal(k, shape, F32) * s
    gain = lambda k, n: 1.0 + 0.02 * jax.random.normal(k, (DEPTH, n), F32)
    P, G, C = SSM_STATE, N_SSM_GROUPS, SSM_GROUP
    a_im = math.pi * jnp.broadcast_to(jnp.arange(P, dtype=F32), (DEPTH, G, P))
    return {
        "x": jax.random.normal(ks[0], (BATCH, SEQ, D_MODEL), F32),
        "ffn1_norm": gain(ks[1], D_MODEL),
        "ffn1_w_gate": nrm(ks[2], (DEPTH, D_MODEL, D_FF), D_MODEL ** -0.5),
        "ffn1_w_up": nrm(ks[3], (DEPTH, D_MODEL, D_FF), D_MODEL ** -0.5),
        "ffn1_w_down": nrm(ks[4], (DEPTH, D_FF, D_MODEL), D_FF ** -0.5),
        "mix_norm": gain(ks[5], D_MODEL),
        "w_in": nrm(ks[6], (DEPTH, D_MODEL, D_IN), D_MODEL ** -0.5),
        "ssm_A_re": -0.5 + 0.01 * jax.random.normal(ks[7], (DEPTH, G, P), F32),
        "ssm_A_im": a_im + 0.01 * jax.random.normal(ks[8], (DEPTH, G, P), F32),
        "ssm_log_dt": jax.random.uniform(ks[9], (DEPTH, G), F32, math.log(DT_MIN), math.log(DT_MAX)),
        "ssm_B_re": nrm(ks[10], (DEPTH, G, P, C), (2 * C) ** -0.5),
        "ssm_B_im": nrm(ks[11], (DEPTH, G, P, C), (2 * C) ** -0.5),
        "ssm_C_re": nrm(ks[12], (DEPTH, G, C, P), (2 * P) ** -0.5),
        "ssm_C_im": nrm(ks[13], (DEPTH, G, C, P), (2 * P) ** -0.5),
        "ssm_D": nrm(ks[14], (DEPTH, D_SSM), 1.0),
        "ssm_w_glu": nrm(ks[15], (DEPTH, D_SSM, D_SSM), D_SSM ** -0.5),
        "ssm_b_glu": nrm(ks[16], (DEPTH, D_SSM), 0.01),
        "attn_sinks": nrm(ks[17], (DEPTH, N_HEADS), 1.0),
        "ssm_out_norm": gain(ks[18], D_SSM),
        "attn_out_norm": gain(ks[19], D_ATTN),
        "w_out": nrm(ks[20], (DEPTH, D_MIX, D_MODEL), D_MIX ** -0.5),
        "ffn2_norm": gain(ks[21], D_MODEL),
        "ffn2_w_gate": nrm(ks[22], (DEPTH, D_MODEL, D_FF), D_MODEL ** -0.5),
        "ffn2_w_up": nrm(ks[23], (DEPTH, D_MODEL, D_FF), D_MODEL ** -0.5),
        "ffn2_w_down": nrm(ks[24], (DEPTH, D_FF, D_MODEL), D_FF ** -0.5),
        "final_norm": 1.0 + 0.02 * jax.random.normal(ks[25], (D_MODEL,), F32),
    }


def reference(x, ffn1_norm, ffn1_w_gate, ffn1_w_up, ffn1_w_down, mix_norm, w_in,
              ssm_A_re, ssm_A_im, ssm_log_dt, ssm_B_re, ssm_B_im, ssm_C_re, ssm_C_im,
              ssm_D, ssm_w_glu, ssm_b_glu, attn_sinks, ssm_out_norm, attn_out_norm,
              w_out, ffn2_norm, ffn2_w_gate, ffn2_w_up, ffn2_w_down, final_norm):
    Bsz, L, _ = x.shape
    cos, sin = rope_tables(L)
    for l in range(DEPTH):
        h = rms_norm(x, ffn1_norm[l])
        x = x + RES_HALF * swiglu(h, ffn1_w_gate[l], ffn1_w_up[l], ffn1_w_down[l])
        h = rms_norm(x, mix_norm[l])
        proj = h @ w_in[l]
        u, q, k, v = jnp.split(proj, [D_SSM, D_SSM + D_ATTN, D_SSM + D_ATTN + D_KV], axis=-1)
        q = partial_rope(q.reshape(Bsz, L, N_HEADS, HEAD_DIM), cos, sin)
        k = partial_rope(k.reshape(Bsz, L, N_KV_HEADS, HEAD_DIM), cos, sin)
        v = v.reshape(Bsz, L, N_KV_HEADS, HEAD_DIM)
        y_ssm = s5_mixer(u, ssm_A_re[l], ssm_A_im[l], ssm_log_dt[l], ssm_B_re[l], ssm_B_im[l],
                         ssm_C_re[l], ssm_C_im[l], ssm_D[l], ssm_w_glu[l], ssm_b_glu[l])
        y_attn = swa_sink_attention(q, k, v, attn_sinks[l])
        y = jnp.concatenate([rms_norm(y_ssm, ssm_out_norm[l]),
                             rms_norm(y_attn, attn_out_norm[l])], axis=-1)
        x = x + y @ w_out[l]
        h = rms_norm(x, ffn2_norm[l])
        x = x + RES_HALF * swiglu(h, ffn2_w_gate[l], ffn2_w_up[l], ffn2_w_down[l])
    return rms_norm(x, final_norm)
```

```python
import functools
import math

import jax
import jax.numpy as jnp
from jax import lax
from jax.experimental import pallas as pl
from jax.experimental.pallas import tpu as pltpu

F32 = jnp.float32
BF16 = jnp.bfloat16

SSM_GROUP = 16
SSM_STATE = 64
HEAD_DIM = 64
N_KV_HEADS = 2
WINDOW = 128
ROPE_DIM = HEAD_DIM // 4
ROPE_THETA = 500000.0
RES_HALF = 0.5
EPS = 1e-6
NEG_INF = -1e30
LANES = 128
SUBLANES = 8
MXU_N = 256

VMEM_LIMIT = 56 * 1024 * 1024


def _rms(x, g):
    return x * lax.rsqrt(jnp.mean(x * x, axis=-1, keepdims=True) + EPS) * g


def _swiglu_residual(x, g_ref, wg_ref, wu_ref, wd_ref, fc):
    hn = _rms(x, g_ref[...]).astype(BF16)
    d_ff = wg_ref.shape[1]
    acc = x
    for c in range(d_ff // fc):
        sl = slice(c * fc, (c + 1) * fc)
        g = jnp.dot(hn, wg_ref[:, sl], preferred_element_type=F32)
        u = jnp.dot(hn, wu_ref[:, sl], preferred_element_type=F32)
        a = (g * jax.nn.sigmoid(g) * u).astype(BF16)
        acc = acc + RES_HALF * jnp.dot(a, wd_ref[sl, :], preferred_element_type=F32)
    return acc


def _rope(t, c, s1, s2):
    half = ROPE_DIM // 2
    return t * c + pltpu.roll(t, half, 1) * s1 + pltpu.roll(t, LANES - half, 1) * s2


def _ffn_in_kernel(x_ref, g1_ref, wg_ref, wu_ref, wd_ref, gm_ref, win_ref,
                   rc_ref, rs1_ref, rs2_ref,
                   x1_ref, u_ref, q_ref, k_ref, v_ref, *, fc, d_ssm, d_attn, d_kv):
    x1 = _swiglu_residual(x_ref[...], g1_ref, wg_ref, wu_ref, wd_ref, fc)
    x1_ref[...] = x1
    hn = _rms(x1, gm_ref[...]).astype(BF16)
    u_ref[...] = jnp.dot(hn, win_ref[:, :d_ssm], preferred_element_type=F32)
    c, s1, s2 = rc_ref[...], rs1_ref[...], rs2_ref[...]
    for j in range(d_attn // LANES):
        lo = d_ssm + j * LANES
        t = jnp.dot(hn, win_ref[:, lo:lo + LANES], preferred_element_type=F32)
        q_ref[:, j * LANES:(j + 1) * LANES] = _rope(t, c, s1, s2).astype(BF16)
    lo = d_ssm + d_attn
    t = jnp.dot(hn, win_ref[:, lo:lo + d_kv], preferred_element_type=F32)
    k_ref[...] = _rope(t, c, s1, s2).astype(BF16)
    lo = lo + d_kv
    v_ref[...] = jnp.dot(hn, win_ref[:, lo:lo + d_kv], preferred_element_type=F32).astype(BF16)


def _ssm_kernel(u_ref, are_ref, aim_ref, bre_ref, bim_ref, cre_ref, cim_ref,
                d_ref, wglu_ref, bglu_ref, gn_ref,
                y_ref, sre_ref, sim_ref, xre_ref, xim_ref, *, tt):
    rows = tt * SUBLANES
    d_ssm = u_ref.shape[-1]
    n_state = are_ref.shape[-1]
    half_c = d_ssm // 2
    half_s = n_state // 2

    @pl.when(pl.program_id(0) == 0)
    def _():
        xre_ref[...] = jnp.zeros_like(xre_ref)
        xim_ref[...] = jnp.zeros_like(xim_ref)

    u = u_ref[...].reshape(rows, d_ssm)
    ub = u.astype(BF16)
    for h in range(2):
        uh = ub[:, h * half_c:(h + 1) * half_c]
        sre_ref[:, h * half_s:(h + 1) * half_s] = jnp.dot(uh, bre_ref[h], preferred_element_type=F32)
        sim_ref[:, h * half_s:(h + 1) * half_s] = jnp.dot(uh, bim_ref[h], preferred_element_type=F32)

    def step(t, carry):
        xr, xi = carry
        r0 = pl.multiple_of(t * SUBLANES, SUBLANES)
        ar = are_ref[...]
        ai = aim_ref[...]
        nr = ar * xr - ai * xi + sre_ref[pl.ds(r0, SUBLANES), :]
        ni = ar * xi + ai * xr + sim_ref[pl.ds(r0, SUBLANES), :]
        sre_ref[pl.ds(r0, SUBLANES), :] = nr
        sim_ref[pl.ds(r0, SUBLANES), :] = ni
        return nr, ni

    xr, xi = lax.fori_loop(0, tt, step, (xre_ref[...], xim_ref[...]))
    xre_ref[...] = xr
    xim_ref[...] = xi

    ys = []
    for h in range(2):
        sr = sre_ref[:, h * half_s:(h + 1) * half_s].astype(BF16)
        si = sim_ref[:, h * half_s:(h + 1) * half_s].astype(BF16)
        ys.append(jnp.dot(sr, cre_ref[h], preferred_element_type=F32)
                  + jnp.dot(si, cim_ref[h], preferred_element_type=F32))
    y = jnp.concatenate(ys, axis=-1) + d_ref[...] * u
    y = jax.nn.gelu(y)
    z = jnp.dot(y.astype(BF16), wglu_ref[...], preferred_element_type=F32) + bglu_ref[...]
    y = y * jax.nn.sigmoid(z)
    y_ref[...] = _rms(y, gn_ref[...]).reshape(tt, SUBLANES, d_ssm)


def _attn_kernel(sink_ref, q_ref, kc_ref, kp_ref, vc_ref, vp_ref, gn_ref, o_ref, *, nq):
    blk = WINDOW
    n = pl.program_id(1)
    lane = lax.broadcasted_iota(jnp.int32, (1, LANES), 1)
    k_all = jnp.concatenate([kp_ref[...], kc_ref[...]], axis=0).astype(F32)
    v_all = jnp.concatenate([vp_ref[...], vc_ref[...]], axis=0).astype(F32)

    def placed(t, kh, half):
        t = jnp.where((lane // HEAD_DIM) == kh, t, 0.0)
        if half != kh:
            t = pltpu.roll(t, HEAD_DIM, 1)
        return t.astype(BF16)

    kmat = [[placed(k_all, kh, half) for half in range(2)] for kh in range(N_KV_HEADS)]
    vmat = [[placed(v_all, kh, half) for half in range(2)] for kh in range(N_KV_HEADS)]

    qi = lax.broadcasted_iota(jnp.int32, (blk, 2 * blk), 0)
    sj = lax.broadcasted_iota(jnp.int32, (blk, 2 * blk), 1)
    band = (sj > qi) & (sj <= qi + blk)
    n_pairs = q_ref.shape[-1] // LANES
    heads_per_kv = (2 * n_pairs) // N_KV_HEADS

    outs = []
    for j in range(nq):
        mask = band & ((n * nq + j - 1) * blk + sj >= 0)
        pair_outs = []
        for pr in range(n_pairs):
            qp = q_ref[j * blk:(j + 1) * blk, pr * LANES:(pr + 1) * LANES]
            acc = None
            for half in range(2):
                h = 2 * pr + half
                kh = h // heads_per_kv
                s = lax.dot_general(qp, kmat[kh][half][j * blk:(j + 2) * blk],
                                    (((1,), (1,)), ((), ())), preferred_element_type=F32)
                s = jnp.where(mask, s, NEG_INF)
                sink = sink_ref[h]
                m = jnp.maximum(jnp.max(s, axis=-1, keepdims=True), sink)
                e = jnp.exp(s - m)
                denom = jnp.sum(e, axis=-1, keepdims=True) + jnp.exp(sink - m)
                p = (e / denom).astype(BF16)
                o = jnp.dot(p, vmat[kh][half][j * blk:(j + 2) * blk], preferred_element_type=F32)
                acc = o if acc is None else acc + o
            pair_outs.append(acc)
        outs.append(jnp.concatenate(pair_outs, axis=-1))
    y = jnp.concatenate(outs, axis=0)
    o_ref[...] = _rms(y, gn_ref[...]).astype(o_ref.dtype)


def _out_ffn_kernel(x1_ref, ys_ref, ya_ref, wo_ref, g2_ref, wg_ref, wu_ref, wd_ref, gf_ref,
                    o_ref, *, fc):
    d_ssm = ys_ref.shape[-1]
    x2 = (x1_ref[...]
          + jnp.dot(ys_ref[...].astype(BF16), wo_ref[:d_ssm, :], preferred_element_type=F32)
          + jnp.dot(ya_ref[...], wo_ref[d_ssm:, :], preferred_element_type=F32))
    x3 = _swiglu_residual(x2, g2_ref, wg_ref, wu_ref, wd_ref, fc)
    o_ref[...] = _rms(x3, gf_ref[...])


def _const_spec(shape):
    nd = len(shape)
    return pl.BlockSpec(shape, lambda *_: (0,) * nd, pipeline_mode=pl.Buffered(1))


def _rope_lane_tables(seq):
    half = ROPE_DIM // 2
    inv_freq = ROPE_THETA ** (-jnp.arange(half, dtype=F32) * 2.0 / ROPE_DIM)
    ang = jnp.arange(seq, dtype=F32)[:, None] * inv_freq[None, :]
    cos, sin = jnp.cos(ang), jnp.sin(ang)
    ones = jnp.ones((seq, HEAD_DIM - ROPE_DIM), F32)
    zeros = jnp.zeros((seq, HEAD_DIM - ROPE_DIM), F32)
    zh = jnp.zeros((seq, half), F32)
    c = jnp.concatenate([cos, cos, ones], axis=-1)
    s1 = jnp.concatenate([zh, sin, zeros], axis=-1)
    s2 = jnp.concatenate([-sin, zh, zeros], axis=-1)
    reps = LANES // HEAD_DIM
    return tuple(jnp.tile(t, (1, reps)) for t in (c, s1, s2))


def _block_diag(m, n_halves):
    g, r, c = m.shape
    gh = g // n_halves
    m = m.reshape(n_halves, gh, r, c)
    eye = jnp.eye(gh, dtype=m.dtype)
    return jnp.einsum('hgrc,gk->hgrkc', m, eye).reshape(n_halves, gh * r, gh * c)


def _ssm_params(a_re, a_im, log_dt, b_re, b_im, c_re, c_im):
    lam = lax.complex(a_re, a_im)
    dt = jnp.exp(log_dt)[:, None]
    lam_bar = jnp.exp(lam * dt)
    zoh = (lam_bar - 1.0) / lam
    b_bar = zoh[..., None] * lax.complex(b_re, b_im)
    n_state = lam_bar.size
    are = jnp.broadcast_to(lam_bar.real.reshape(1, n_state), (SUBLANES, n_state))
    aim = jnp.broadcast_to(lam_bar.imag.reshape(1, n_state), (SUBLANES, n_state))
    bre = _block_diag(jnp.swapaxes(b_bar.real, 1, 2), 2).astype(BF16)
    bim = _block_diag(jnp.swapaxes(b_bar.imag, 1, 2), 2).astype(BF16)
    cre = _block_diag(jnp.swapaxes(c_re, 1, 2), 2).astype(BF16)
    cim = _block_diag(jnp.swapaxes(-c_im, 1, 2), 2).astype(BF16)
    return are, aim, bre, bim, cre, cim


def _layer(x, cs, ffn1_norm, wg1, wu1, wd1, mix_norm, w_in, ssm_A_re, ssm_A_im, ssm_log_dt,
           ssm_B_re, ssm_B_im, ssm_C_re, ssm_C_im, ssm_D, ssm_w_glu, ssm_b_glu, attn_sinks,
           ssm_out_norm, attn_out_norm, w_out, ffn2_norm, wg2, wu2, wd2, final_norm):
    bsz, seq, d_model = x.shape
    d_ff = wg1.shape[1]
    d_ssm = ssm_D.shape[0]
    d_attn = attn_out_norm.shape[0]
    d_kv = N_KV_HEADS * HEAD_DIM
    d_in = w_in.shape[1]
    tm = 512
    fc = MXU_N
    tt = 64
    nq = 4
    row = lambda v: v.reshape(1, -1).astype(F32)

    col_scale = jnp.concatenate([jnp.ones((d_ssm,), F32),
                                 jnp.full((d_attn,), 1.0 / math.sqrt(HEAD_DIM), F32),
                                 jnp.ones((2 * d_kv,), F32)])
    w_in_b = (w_in * col_scale[None, :]).astype(BF16)

    cparams = lambda sem: pltpu.CompilerParams(dimension_semantics=sem, vmem_limit_bytes=VMEM_LIMIT)

    tile = lambda w: pl.BlockSpec((None, tm, w), lambda b, i: (b, i, 0))
    rope_spec = pl.BlockSpec((tm, LANES), lambda b, i: (i, 0))
    x1, u_tb, q, k, v = pl.pallas_call(
        functools.partial(_ffn_in_kernel, fc=fc, d_ssm=d_ssm, d_attn=d_attn, d_kv=d_kv),
        grid=(bsz, seq // tm),
        in_specs=[tile(d_model), _const_spec((1, d_model)), _const_spec((d_model, d_ff)),
                  _const_spec((d_model, d_ff)), _const_spec((d_ff, d_model)),
                  _const_spec((1, d_model)), _const_spec((d_model, d_in)),
                  rope_spec, rope_spec, rope_spec],
        out_specs=[tile(d_model),
                   pl.BlockSpec((tm, d_ssm), lambda b, i: (i, b)),
                   tile(d_attn), tile(d_kv), tile(d_kv)],
        out_shape=[jax.ShapeDtypeStruct((bsz, seq, d_model), F32),
                   jax.ShapeDtypeStruct((seq, bsz * d_ssm), F32),
                   jax.ShapeDtypeStruct((bsz, seq, d_attn), BF16),
                   jax.ShapeDtypeStruct((bsz, seq, d_kv), BF16),
                   jax.ShapeDtypeStruct((bsz, seq, d_kv), BF16)],
        compiler_params=cparams(("parallel", "parallel")),
        name="ffn_in",
    )(x, row(ffn1_norm), wg1.astype(BF16), wu1.astype(BF16), wd1.astype(BF16),
      row(mix_norm), w_in_b, *cs)

    assert bsz == SUBLANES
    are, aim, bre, bim, cre, cim = _ssm_params(ssm_A_re, ssm_A_im, ssm_log_dt,
                                               ssm_B_re, ssm_B_im, ssm_C_re, ssm_C_im)
    n_state = are.shape[-1]
    tb_spec = pl.BlockSpec((tt, bsz, d_ssm), lambda i: (i, 0, 0))
    y_ssm_tb = pl.pallas_call(
        functools.partial(_ssm_kernel, tt=tt),
        grid=(seq // tt,),
        in_specs=[tb_spec, _const_spec(are.shape), _const_spec(aim.shape),
                  _const_spec(bre.shape), _const_spec(bim.shape),
                  _const_spec(cre.shape), _const_spec(cim.shape),
                  _const_spec((1, d_ssm)), _const_spec((d_ssm, d_ssm)),
                  _const_spec((1, d_ssm)), _const_spec((1, d_ssm))],
        out_specs=tb_spec,
        out_shape=jax.ShapeDtypeStruct((seq, bsz, d_ssm), F32),
        scratch_shapes=[pltpu.VMEM((tt * bsz, n_state), F32), pltpu.VMEM((tt * bsz, n_state), F32),
                        pltpu.VMEM((bsz, n_state), F32), pltpu.VMEM((bsz, n_state), F32)],
        compiler_params=cparams(("arbitrary",)),
        name="ssm",
    )(u_tb.reshape(seq, bsz, d_ssm), are, aim, bre, bim, cre, cim,
      row(ssm_D), ssm_w_glu.astype(BF16), row(ssm_b_glu), row(ssm_out_norm))

    tq = nq * WINDOW
    cur = lambda w: pl.BlockSpec((None, tq, w), lambda b, n: (b, n, 0))
    prev = lambda w: pl.BlockSpec((None, WINDOW, w), lambda b, n: (b, jnp.maximum(n * nq - 1, 0), 0))
    y_attn = pl.pallas_call(
        functools.partial(_attn_kernel, nq=nq),
        grid=(bsz, seq // tq),
        in_specs=[pl.BlockSpec(memory_space=pltpu.SMEM),
                  cur(d_attn), cur(d_kv), prev(d_kv), cur(d_kv), prev(d_kv),
                  pl.BlockSpec((1, d_attn), lambda b, n: (0, 0))],
        out_specs=cur(d_attn),
        out_shape=jax.ShapeDtypeStruct((bsz, seq, d_attn), BF16),
        compiler_params=cparams(("parallel", "parallel")),
        name="attn",
    )(attn_sinks.astype(F32), q, k, k, v, v, row(attn_out_norm))

    out = pl.pallas_call(
        functools.partial(_out_ffn_kernel, fc=fc),
        grid=(bsz, seq // tm),
        in_specs=[tile(d_model),
                  pl.BlockSpec((tm, d_ssm), lambda b, i: (i, b)),
                  tile(d_attn), _const_spec((d_ssm + d_attn, d_model)),
                  _const_spec((1, d_model)), _const_spec((d_model, d_ff)),
                  _const_spec((d_model, d_ff)), _const_spec((d_ff, d_model)),
                  _const_spec((1, d_model))],
        out_specs=tile(d_model),
        out_shape=jax.ShapeDtypeStruct((bsz, seq, d_model), F32),
        compiler_params=cparams(("parallel", "parallel")),
        name="out_ffn",
    )(x1, y_ssm_tb.reshape(seq, bsz * d_ssm), y_attn, w_out.astype(BF16),
      row(ffn2_norm), wg2.astype(BF16), wu2.astype(BF16), wd2.astype(BF16), row(final_norm))
    return out


def kernel(x, ffn1_norm, ffn1_w_gate, ffn1_w_up, ffn1_w_down, mix_norm, w_in, ssm_A_re, ssm_A_im, ssm_log_dt, ssm_B_re, ssm_B_im, ssm_C_re, ssm_C_im, ssm_D, ssm_w_glu, ssm_b_glu, attn_sinks, ssm_out_norm, attn_out_norm, w_out, ffn2_norm, ffn2_w_gate, ffn2_w_up, ffn2_w_down, final_norm):
    depth = ffn1_norm.shape[0]
    assert depth == 1, "the final norm is fused into the last layer's kernel"
    cs = _rope_lane_tables(x.shape[1])
    l = 0
    return _layer(x, cs, ffn1_norm[l], ffn1_w_gate[l], ffn1_w_up[l], ffn1_w_down[l], mix_norm[l],
                  w_in[l], ssm_A_re[l], ssm_A_im[l], ssm_log_dt[l], ssm_B_re[l], ssm_B_im[l],
                  ssm_C_re[l], ssm_C_im[l], ssm_D[l], ssm_w_glu[l], ssm_b_glu[l], attn_sinks[l],
                  ssm_out_norm[l], attn_out_norm[l], w_out[l], ffn2_norm[l], ffn2_w_gate[l],
                  ffn2_w_up[l], ffn2_w_down[l], final_norm)
```

```python
import functools
import math

import jax
import jax.numpy as jnp
from jax import lax
from jax.experimental import pallas as pl
from jax.experimental.pallas import tpu as pltpu

F32 = jnp.float32
BF16 = jnp.bfloat16

SSM_GROUP = 16
SSM_STATE = 64
HEAD_DIM = 64
N_KV_HEADS = 2
WINDOW = 128
ROPE_DIM = HEAD_DIM // 4
ROPE_THETA = 500000.0
RES_HALF = 0.5
EPS = 1e-6
NEG_INF = -1e30
LOG2E = math.log2(math.e)
LANES = 128
SUBLANES = 8
MXU_N = 256

VMEM_LIMIT = 56 * 1024 * 1024


def _rms(x, g):
    return x * lax.rsqrt(jnp.mean(x * x, axis=-1, keepdims=True) + EPS) * g


def _swiglu_residual(x, g_ref, wg_ref, wu_ref, wd_ref, fc):
    hn = _rms(x, g_ref[...]).astype(BF16)
    d_ff = wg_ref.shape[1]
    acc = x
    for c in range(d_ff // fc):
        sl = slice(c * fc, (c + 1) * fc)
        g = jnp.dot(hn, wg_ref[:, sl], preferred_element_type=F32)
        u = jnp.dot(hn, wu_ref[:, sl], preferred_element_type=F32)
        a = (g * jax.nn.sigmoid(g) * u).astype(BF16)
        acc = acc + RES_HALF * jnp.dot(a, wd_ref[sl, :], preferred_element_type=F32)
    return acc


def _rope(t, c, s1, s2):
    half = ROPE_DIM // 2
    return t * c + pltpu.roll(t, half, 1) * s1 + pltpu.roll(t, LANES - half, 1) * s2


def _ffn_in_kernel(x_ref, g1_ref, wg_ref, wu_ref, wd_ref, gm_ref, win_ref,
                   rc_ref, rs1_ref, rs2_ref,
                   x1_ref, u_ref, q_ref, k_ref, v_ref, *, fc, d_ssm, d_attn, d_kv):
    x1 = _swiglu_residual(x_ref[...], g1_ref, wg_ref, wu_ref, wd_ref, fc)
    x1_ref[...] = x1
    hn = _rms(x1, gm_ref[...]).astype(BF16)
    u_ref[...] = jnp.dot(hn, win_ref[:, :d_ssm], preferred_element_type=F32)
    c, s1, s2 = rc_ref[...], rs1_ref[...], rs2_ref[...]
    qkv = jnp.dot(hn, win_ref[:, d_ssm:], preferred_element_type=F32)
    for j in range(d_attn // LANES):
        sl = slice(j * LANES, (j + 1) * LANES)
        q_ref[:, sl] = _rope(qkv[:, sl], c, s1, s2).astype(BF16)
    k_ref[...] = _rope(qkv[:, d_attn:d_attn + d_kv], c, s1, s2).astype(BF16)
    v_ref[...] = qkv[:, d_attn + d_kv:].astype(BF16)


def _ssm_kernel(u_ref, are_ref, aim_ref, bre_ref, bim_ref, cre_ref, cim_ref,
                d_ref, wglu_ref, bglu_ref, gn_ref,
                y_ref, xre_ref, xim_ref, *s_refs, tt, n_chain):
    sre, sim = s_refs[:n_chain], s_refs[n_chain:]
    rows = tt * SUBLANES
    d_ssm = u_ref.shape[-1]
    cw = d_ssm // n_chain
    sw = are_ref.shape[-1] // n_chain

    @pl.when(pl.program_id(0) == 0)
    def _():
        xre_ref[...] = jnp.zeros_like(xre_ref)
        xim_ref[...] = jnp.zeros_like(xim_ref)

    u = u_ref[...].reshape(rows, d_ssm)
    ub = u.astype(BF16)
    for c in range(n_chain):
        uc = ub[:, c * cw:(c + 1) * cw]
        sre[c][...] = jnp.dot(uc, bre_ref[c], preferred_element_type=F32)
        sim[c][...] = jnp.dot(uc, bim_ref[c], preferred_element_type=F32)

    ys = []
    for c in range(n_chain):
        st = slice(c * sw, (c + 1) * sw)
        ar, ai = are_ref[:, st], aim_ref[:, st]
        xr, xi = xre_ref[:, st], xim_ref[:, st]
        for t in range(tt):
            r = slice(t * SUBLANES, (t + 1) * SUBLANES)
            xr, xi = (ar * xr - ai * xi + sre[c][r, :],
                      ar * xi + ai * xr + sim[c][r, :])
            sre[c][r, :] = xr
            sim[c][r, :] = xi
        xre_ref[:, st] = xr
        xim_ref[:, st] = xi
        ys.append(jnp.dot(sre[c][...].astype(BF16), cre_ref[c], preferred_element_type=F32)
                  + jnp.dot(sim[c][...].astype(BF16), cim_ref[c], preferred_element_type=F32))
    y = jnp.concatenate(ys, axis=-1) + d_ref[...] * u
    y = jax.nn.gelu(y)
    z = jnp.dot(y.astype(BF16), wglu_ref[...], preferred_element_type=F32) + bglu_ref[...]
    y = y * jax.nn.sigmoid(z)
    y_ref[...] = _rms(y, gn_ref[...]).reshape(tt, SUBLANES, d_ssm)


def _attn_kernel(sink_ref, q_ref, kc_ref, kp_ref, vc_ref, vp_ref, gn_ref, o_ref, *, nq):
    blk = WINDOW
    n = pl.program_id(1)
    lane = lax.broadcasted_iota(jnp.int32, (1, LANES), 1)
    k_all = jnp.concatenate([kp_ref[...], kc_ref[...]], axis=0).astype(F32)
    v_all = jnp.concatenate([vp_ref[...], vc_ref[...]], axis=0).astype(F32)

    def placed(t, kh, half):
        t = jnp.where((lane // HEAD_DIM) == kh, t, 0.0)
        if half != kh:
            t = pltpu.roll(t, HEAD_DIM, 1)
        return t

    bf_tile = 2 * SUBLANES
    row0 = lax.broadcasted_iota(jnp.int32, (bf_tile, 1), 0) == 0
    kmat, vext, vhead = [], [], []
    for kh in range(N_KV_HEADS):
        kmat.append([]), vext.append([]), vhead.append([])
        for half in range(2):
            ones = jnp.broadcast_to(jnp.where((lane // HEAD_DIM) == half, 1.0, 0.0), v_all.shape)
            vp = placed(v_all, kh, half)
            kmat[kh].append(placed(k_all, kh, half).astype(BF16))
            vext[kh].append(jnp.concatenate([vp, ones], axis=-1).astype(BF16))
            vhead[kh].append([
                jnp.concatenate([jnp.where(row0, 0.0, vp[j * blk:j * blk + bf_tile]),
                                 ones[:bf_tile]], axis=-1).astype(BF16)
                for j in range(nq)])

    qi = lax.broadcasted_iota(jnp.int32, (blk, 2 * blk), 0)
    sj = lax.broadcasted_iota(jnp.int32, (blk, 2 * blk), 1)
    band = (sj > qi) & (sj <= qi + blk)
    band_first = band & ((sj >= blk) | (n > 0))
    col0 = sj == 0
    n_pairs = q_ref.shape[-1] // LANES
    heads_per_kv = (2 * n_pairs) // N_KV_HEADS

    outs = []
    for j in range(nq):
        valid = band_first if j == 0 else band
        pair_outs = []
        for pr in range(n_pairs):
            qp = q_ref[j * blk:(j + 1) * blk, pr * LANES:(pr + 1) * LANES]
            res = None
            for half in range(2):
                h = 2 * pr + half
                kh = h // heads_per_kv
                fill = jnp.where(col0, sink_ref[h] * LOG2E, NEG_INF)
                s = lax.dot_general(qp, kmat[kh][half][j * blk:(j + 2) * blk],
                                    (((1,), (1,)), ((), ())), preferred_element_type=F32)
                s = jnp.where(valid, s, fill)
                m = jnp.max(s, axis=-1, keepdims=True)
                p = jnp.exp2(s - m).astype(BF16)
                win = jnp.concatenate([vhead[kh][half][j],
                                       vext[kh][half][j * blk + bf_tile:(j + 2) * blk]], axis=0)
                o = jnp.dot(p, win, preferred_element_type=F32)
                res = o if res is None else res + o
            pair_outs.append(res[:, :LANES] / res[:, LANES:])
        outs.append(jnp.concatenate(pair_outs, axis=-1))
    y = jnp.concatenate(outs, axis=0)
    o_ref[...] = _rms(y, gn_ref[...]).astype(o_ref.dtype)


def _out_ffn_kernel(x1_ref, ys_ref, ya_ref, wo_ref, g2_ref, wg_ref, wu_ref, wd_ref, gf_ref,
                    o_ref, *, fc):
    d_ssm = ys_ref.shape[-1]
    x2 = (x1_ref[...]
          + jnp.dot(ys_ref[...].astype(BF16), wo_ref[:d_ssm, :], preferred_element_type=F32)
          + jnp.dot(ya_ref[...], wo_ref[d_ssm:, :], preferred_element_type=F32))
    x3 = _swiglu_residual(x2, g2_ref, wg_ref, wu_ref, wd_ref, fc)
    o_ref[...] = _rms(x3, gf_ref[...])


def _const_spec(shape):
    nd = len(shape)
    return pl.BlockSpec(shape, lambda *_: (0,) * nd, pipeline_mode=pl.Buffered(1))


def _rope_lane_tables(seq):
    half = ROPE_DIM // 2
    inv_freq = ROPE_THETA ** (-jnp.arange(half, dtype=F32) * 2.0 / ROPE_DIM)
    ang = jnp.arange(seq, dtype=F32)[:, None] * inv_freq[None, :]
    cos, sin = jnp.cos(ang), jnp.sin(ang)
    ones = jnp.ones((seq, HEAD_DIM - ROPE_DIM), F32)
    zeros = jnp.zeros((seq, HEAD_DIM - ROPE_DIM), F32)
    zh = jnp.zeros((seq, half), F32)
    c = jnp.concatenate([cos, cos, ones], axis=-1)
    s1 = jnp.concatenate([zh, sin, zeros], axis=-1)
    s2 = jnp.concatenate([-sin, zh, zeros], axis=-1)
    reps = LANES // HEAD_DIM
    return tuple(jnp.tile(t, (1, reps)) for t in (c, s1, s2))


def _block_diag(m, n_halves):
    g, r, c = m.shape
    gh = g // n_halves
    m = m.reshape(n_halves, gh, r, c)
    eye = jnp.eye(gh, dtype=m.dtype)
    return jnp.einsum('hgrc,gk->hgrkc', m, eye).reshape(n_halves, gh * r, gh * c)


def _ssm_params(a_re, a_im, log_dt, b_re, b_im, c_re, c_im, n_chain):
    lam = lax.complex(a_re, a_im)
    dt = jnp.exp(log_dt)[:, None]
    lam_bar = jnp.exp(lam * dt)
    zoh = (lam_bar - 1.0) / lam
    b_bar = zoh[..., None] * lax.complex(b_re, b_im)
    n_state = lam_bar.size
    are = jnp.broadcast_to(lam_bar.real.reshape(1, n_state), (SUBLANES, n_state))
    aim = jnp.broadcast_to(lam_bar.imag.reshape(1, n_state), (SUBLANES, n_state))
    bre = _block_diag(jnp.swapaxes(b_bar.real, 1, 2), n_chain).astype(BF16)
    bim = _block_diag(jnp.swapaxes(b_bar.imag, 1, 2), n_chain).astype(BF16)
    cre = _block_diag(jnp.swapaxes(c_re, 1, 2), n_chain).astype(BF16)
    cim = _block_diag(jnp.swapaxes(-c_im, 1, 2), n_chain).astype(BF16)
    return are, aim, bre, bim, cre, cim


def _layer(x, cs, ffn1_norm, wg1, wu1, wd1, mix_norm, w_in, ssm_A_re, ssm_A_im, ssm_log_dt,
           ssm_B_re, ssm_B_im, ssm_C_re, ssm_C_im, ssm_D, ssm_w_glu, ssm_b_glu, attn_sinks,
           ssm_out_norm, attn_out_norm, w_out, ffn2_norm, wg2, wu2, wd2, final_norm):
    bsz, seq, d_model = x.shape
    d_ff = wg1.shape[1]
    d_ssm = ssm_D.shape[0]
    d_attn = attn_out_norm.shape[0]
    d_kv = N_KV_HEADS * HEAD_DIM
    d_in = w_in.shape[1]
    tm = 512
    fc = MXU_N
    tt = 64
    n_chain = 4
    nq = 4
    row = lambda v: v.reshape(1, -1).astype(F32)

    col_scale = jnp.concatenate([jnp.ones((d_ssm,), F32),
                                 jnp.full((d_attn,), LOG2E / math.sqrt(HEAD_DIM), F32),
                                 jnp.ones((2 * d_kv,), F32)])
    w_in_b = (w_in * col_scale[None, :]).astype(BF16)

    cparams = lambda sem: pltpu.CompilerParams(dimension_semantics=sem, vmem_limit_bytes=VMEM_LIMIT)

    tile = lambda w: pl.BlockSpec((None, tm, w), lambda b, i: (b, i, 0))
    rope_spec = pl.BlockSpec((tm, LANES), lambda b, i: (i, 0))
    x1, u_tb, q, k, v = pl.pallas_call(
        functools.partial(_ffn_in_kernel, fc=fc, d_ssm=d_ssm, d_attn=d_attn, d_kv=d_kv),
        grid=(bsz, seq // tm),
        in_specs=[tile(d_model), _const_spec((1, d_model)), _const_spec((d_model, d_ff)),
                  _const_spec((d_model, d_ff)), _const_spec((d_ff, d_model)),
                  _const_spec((1, d_model)), _const_spec((d_model, d_in)),
                  rope_spec, rope_spec, rope_spec],
        out_specs=[tile(d_model),
                   pl.BlockSpec((tm, d_ssm), lambda b, i: (i, b)),
                   tile(d_attn), tile(d_kv), tile(d_kv)],
        out_shape=[jax.ShapeDtypeStruct((bsz, seq, d_model), F32),
                   jax.ShapeDtypeStruct((seq, bsz * d_ssm), F32),
                   jax.ShapeDtypeStruct((bsz, seq, d_attn), BF16),
                   jax.ShapeDtypeStruct((bsz, seq, d_kv), BF16),
                   jax.ShapeDtypeStruct((bsz, seq, d_kv), BF16)],
        compiler_params=cparams(("parallel", "parallel")),
        name="ffn_in",
    )(x, row(ffn1_norm), wg1.astype(BF16), wu1.astype(BF16), wd1.astype(BF16),
      row(mix_norm), w_in_b, *cs)

    assert bsz == SUBLANES
    are, aim, bre, bim, cre, cim = _ssm_params(ssm_A_re, ssm_A_im, ssm_log_dt,
                                               ssm_B_re, ssm_B_im, ssm_C_re, ssm_C_im, n_chain)
    n_state = are.shape[-1]
    tb_spec = pl.BlockSpec((tt, bsz, d_ssm), lambda i: (i, 0, 0))
    y_ssm_tb = pl.pallas_call(
        functools.partial(_ssm_kernel, tt=tt, n_chain=n_chain),
        grid=(seq // tt,),
        in_specs=[tb_spec, _const_spec(are.shape), _const_spec(aim.shape),
                  _const_spec(bre.shape), _const_spec(bim.shape),
                  _const_spec(cre.shape), _const_spec(cim.shape),
                  _const_spec((1, d_ssm)), _const_spec((d_ssm, d_ssm)),
                  _const_spec((1, d_ssm)), _const_spec((1, d_ssm))],
        out_specs=tb_spec,
        out_shape=jax.ShapeDtypeStruct((seq, bsz, d_ssm), F32),
        scratch_shapes=([pltpu.VMEM((bsz, n_state), F32)] * 2
                        + [pltpu.VMEM((tt * bsz, n_state // n_chain), F32)] * (2 * n_chain)),
        compiler_params=cparams(("arbitrary",)),
        name="ssm",
    )(u_tb.reshape(seq, bsz, d_ssm), are, aim, bre, bim, cre, cim,
      row(ssm_D), ssm_w_glu.astype(BF16), row(ssm_b_glu), row(ssm_out_norm))

    tq = nq * WINDOW
    cur = lambda w: pl.BlockSpec((None, tq, w), lambda b, n: (b, n, 0))
    prev = lambda w: pl.BlockSpec((None, WINDOW, w), lambda b, n: (b, jnp.maximum(n * nq - 1, 0), 0))
    y_attn = pl.pallas_call(
        functools.partial(_attn_kernel, nq=nq),
        grid=(bsz, seq // tq),
        in_specs=[pl.BlockSpec(memory_space=pltpu.SMEM),
                  cur(d_attn), cur(d_kv), prev(d_kv), cur(d_kv), prev(d_kv),
                  pl.BlockSpec((1, d_attn), lambda b, n: (0, 0))],
        out_specs=cur(d_attn),
        out_shape=jax.ShapeDtypeStruct((bsz, seq, d_attn), BF16),
        compiler_params=cparams(("parallel", "parallel")),
        name="attn",
    )(attn_sinks.astype(F32), q, k, k, v, v, row(attn_out_norm))

    out = pl.pallas_call(
        functools.partial(_out_ffn_kernel, fc=fc),
        grid=(bsz, seq // tm),
        in_specs=[tile(d_model),
                  pl.BlockSpec((tm, d_ssm), lambda b, i: (i, b)),
                  tile(d_attn), _const_spec((d_ssm + d_attn, d_model)),
                  _const_spec((1, d_model)), _const_spec((d_model, d_ff)),
                  _const_spec((d_model, d_ff)), _const_spec((d_ff, d_model)),
                  _const_spec((1, d_model))],
        out_specs=tile(d_model),
        out_shape=jax.ShapeDtypeStruct((bsz, seq, d_model), F32),
        compiler_params=cparams(("parallel", "parallel")),
        name="out_ffn",
    )(x1, y_ssm_tb.reshape(seq, bsz * d_ssm), y_attn, w_out.astype(BF16),
      row(ffn2_norm), wg2.astype(BF16), wu2.astype(BF16), wd2.astype(BF16), row(final_norm))
    return out


def kernel(x, ffn1_norm, ffn1_w_gate, ffn1_w_up, ffn1_w_down, mix_norm, w_in, ssm_A_re, ssm_A_im, ssm_log_dt, ssm_B_re, ssm_B_im, ssm_C_re, ssm_C_im, ssm_D, ssm_w_glu, ssm_b_glu, attn_sinks, ssm_out_norm, attn_out_norm, w_out, ffn2_norm, ffn2_w_gate, ffn2_w_up, ffn2_w_down, final_norm):
    depth = ffn1_norm.shape[0]
    assert depth == 1, "the final norm is fused into the last layer's kernel"
    cs = _rope_lane_tables(x.shape[1])
    l = 0
    return _layer(x, cs, ffn1_norm[l], ffn1_w_gate[l], ffn1_w_up[l], ffn1_w_down[l], mix_norm[l],
                  w_in[l], ssm_A_re[l], ssm_A_im[l], ssm_log_dt[l], ssm_B_re[l], ssm_B_im[l],
                  ssm_C_re[l], ssm_C_im[l], ssm_D[l], ssm_w_glu[l], ssm_b_glu[l], attn_sinks[l],
                  ssm_out_norm[l], attn_out_norm[l], w_out[l], ffn2_norm[l], ffn2_w_gate[l],
                  ffn2_w_up[l], ffn2_w_down[l], final_norm)
```

```python
import functools
import math

import jax
import jax.numpy as jnp
from jax import lax
from jax.experimental import pallas as pl
from jax.experimental.pallas import tpu as pltpu

F32 = jnp.float32
BF16 = jnp.bfloat16

SSM_GROUP = 16
SSM_STATE = 64
HEAD_DIM = 64
N_KV_HEADS = 2
WINDOW = 128
ROPE_DIM = HEAD_DIM // 4
ROPE_THETA = 500000.0
RES_HALF = 0.5
EPS = 1e-6
NEG_INF = -1e30
LOG2E = math.log2(math.e)
LANES = 128
SUBLANES = 8
MXU_N = 256

VMEM_LIMIT = 56 * 1024 * 1024


def _rms(x, g):
    return x * lax.rsqrt(jnp.mean(x * x, axis=-1, keepdims=True) + EPS) * g


def _swiglu_residual(x, g_ref, wgu_ref, wd_ref, fc):
    hn = _rms(x, g_ref[...]).astype(BF16)
    acc = x
    for c in range(wd_ref.shape[0] // fc):
        gu = jnp.dot(hn, wgu_ref[:, 2 * c * fc:2 * (c + 1) * fc], preferred_element_type=F32)
        g, u = gu[:, :fc], gu[:, fc:]
        a = (g * jax.nn.sigmoid(g) * u).astype(BF16)
        acc = acc + RES_HALF * jnp.dot(a, wd_ref[c * fc:(c + 1) * fc, :], preferred_element_type=F32)
    return acc


def _rope(t, c, s1, s2):
    half = ROPE_DIM // 2
    return t * c + pltpu.roll(t, half, 1) * s1 + pltpu.roll(t, LANES - half, 1) * s2


def _ffn_in_kernel(x_ref, g1_ref, wgu_ref, wd_ref, gm_ref, win_ref,
                   rc_ref, rs1_ref, rs2_ref,
                   x1_ref, u_ref, q_ref, k_ref, v_ref, *, fc, sub, d_ssm, d_attn, d_kv):
    for r0 in range(0, x_ref.shape[0], sub):
        rs = slice(r0, r0 + sub)
        x1 = _swiglu_residual(x_ref[rs, :], g1_ref, wgu_ref, wd_ref, fc)
        x1_ref[rs, :] = x1
        hn = _rms(x1, gm_ref[...]).astype(BF16)
        u_ref[rs, :] = jnp.dot(hn, win_ref[:, :d_ssm], preferred_element_type=F32)
        c, s1, s2 = rc_ref[rs, :], rs1_ref[rs, :], rs2_ref[rs, :]
        qkv = jnp.dot(hn, win_ref[:, d_ssm:], preferred_element_type=F32)
        for j in range(d_attn // LANES):
            sl = slice(j * LANES, (j + 1) * LANES)
            q_ref[rs, sl] = _rope(qkv[:, sl], c, s1, s2).astype(BF16)
        k_ref[rs, :] = _rope(qkv[:, d_attn:d_attn + d_kv], c, s1, s2).astype(BF16)
        v_ref[rs, :] = qkv[:, d_attn + d_kv:].astype(BF16)


def _ssm_kernel(u_ref, are_ref, aim_ref, bre_ref, bim_ref, cre_ref, cim_ref,
                d_ref, wglu_ref, bglu_ref, gn_ref,
                y_ref, xre_ref, xim_ref, *s_refs, tt, n_chain):
    sre, sim = s_refs[:n_chain], s_refs[n_chain:]
    rows = tt * SUBLANES
    d_ssm = u_ref.shape[-1]
    cw = d_ssm // n_chain
    sw = are_ref.shape[-1] // n_chain

    @pl.when(pl.program_id(0) == 0)
    def _():
        xre_ref[...] = jnp.zeros_like(xre_ref)
        xim_ref[...] = jnp.zeros_like(xim_ref)

    u = u_ref[...].reshape(rows, d_ssm)
    ub = u.astype(BF16)
    for c in range(n_chain):
        uc = ub[:, c * cw:(c + 1) * cw]
        sre[c][...] = jnp.dot(uc, bre_ref[c], preferred_element_type=F32)
        sim[c][...] = jnp.dot(uc, bim_ref[c], preferred_element_type=F32)

    ys = []
    for c in range(n_chain):
        st = slice(c * sw, (c + 1) * sw)
        ar, ai = are_ref[:, st], aim_ref[:, st]
        xr, xi = xre_ref[:, st], xim_ref[:, st]
        for t in range(tt):
            r = slice(t * SUBLANES, (t + 1) * SUBLANES)
            xr, xi = (ar * xr - ai * xi + sre[c][r, :],
                      ar * xi + ai * xr + sim[c][r, :])
            sre[c][r, :] = xr
            sim[c][r, :] = xi
        xre_ref[:, st] = xr
        xim_ref[:, st] = xi
        ys.append(jnp.dot(sre[c][...].astype(BF16), cre_ref[c], preferred_element_type=F32)
                  + jnp.dot(sim[c][...].astype(BF16), cim_ref[c], preferred_element_type=F32))
    y = jnp.concatenate(ys, axis=-1) + d_ref[...] * u
    y = jax.nn.gelu(y)
    z = jnp.dot(y.astype(BF16), wglu_ref[...], preferred_element_type=F32) + bglu_ref[...]
    y = y * jax.nn.sigmoid(z)
    y_ref[...] = _rms(y, gn_ref[...]).reshape(tt, SUBLANES, d_ssm)


def _attn_kernel(sink_ref, q_ref, kc_ref, kp_ref, vc_ref, vp_ref, gn_ref, o_ref, *, nq):
    blk = WINDOW
    n = pl.program_id(1)
    lane = lax.broadcasted_iota(jnp.int32, (1, LANES), 1)
    k_all = jnp.concatenate([kp_ref[...], kc_ref[...]], axis=0).astype(F32)
    v_all = jnp.concatenate([vp_ref[...], vc_ref[...]], axis=0).astype(F32)

    def placed(t, kh, half):
        t = jnp.where((lane // HEAD_DIM) == kh, t, 0.0)
        if half != kh:
            t = pltpu.roll(t, HEAD_DIM, 1)
        return t

    bf_tile = 2 * SUBLANES
    row0 = lax.broadcasted_iota(jnp.int32, (bf_tile, 1), 0) == 0
    kmat, vext, vhead = [], [], []
    for kh in range(N_KV_HEADS):
        kmat.append([]), vext.append([]), vhead.append([])
        for half in range(2):
            ones = jnp.broadcast_to(jnp.where((lane // HEAD_DIM) == half, 1.0, 0.0), v_all.shape)
            vp = placed(v_all, kh, half)
            kmat[kh].append(placed(k_all, kh, half).astype(BF16))
            vext[kh].append(jnp.concatenate([vp, ones], axis=-1).astype(BF16))
            vhead[kh].append([
                jnp.concatenate([jnp.where(row0, 0.0, vp[j * blk:j * blk + bf_tile]),
                                 ones[:bf_tile]], axis=-1).astype(BF16)
                for j in range(nq)])

    qi = lax.broadcasted_iota(jnp.int32, (blk, 2 * blk), 0)
    sj = lax.broadcasted_iota(jnp.int32, (blk, 2 * blk), 1)
    band = (sj > qi) & (sj <= qi + blk)
    band_first = band & ((sj >= blk) | (n > 0))
    col0 = sj == 0
    n_pairs = q_ref.shape[-1] // LANES
    heads_per_kv = (2 * n_pairs) // N_KV_HEADS

    outs = []
    for j in range(nq):
        valid = band_first if j == 0 else band
        pair_outs = []
        for pr in range(n_pairs):
            qp = q_ref[j * blk:(j + 1) * blk, pr * LANES:(pr + 1) * LANES]
            res = None
            for half in range(2):
                h = 2 * pr + half
                kh = h // heads_per_kv
                fill = jnp.where(col0, sink_ref[h] * LOG2E, NEG_INF)
                s = lax.dot_general(qp, kmat[kh][half][j * blk:(j + 2) * blk],
                                    (((1,), (1,)), ((), ())), preferred_element_type=F32)
                s = jnp.where(valid, s, fill)
                m = jnp.max(s, axis=-1, keepdims=True)
                p = jnp.exp2(s - m).astype(BF16)
                win = jnp.concatenate([vhead[kh][half][j],
                                       vext[kh][half][j * blk + bf_tile:(j + 2) * blk]], axis=0)
                o = jnp.dot(p, win, preferred_element_type=F32)
                res = o if res is None else res + o
            pair_outs.append(res[:, :LANES] / res[:, LANES:])
        outs.append(jnp.concatenate(pair_outs, axis=-1))
    y = jnp.concatenate(outs, axis=0)
    o_ref[...] = _rms(y, gn_ref[...]).astype(o_ref.dtype)


def _out_ffn_kernel(x1_ref, ys_ref, ya_ref, wo_ref, g2_ref, wgu_ref, wd_ref, gf_ref,
                    o_ref, *, fc, sub):
    d_ssm = ys_ref.shape[-1]
    for r0 in range(0, x1_ref.shape[0], sub):
        rs = slice(r0, r0 + sub)
        x2 = (x1_ref[rs, :]
              + jnp.dot(ys_ref[rs, :].astype(BF16), wo_ref[:d_ssm, :], preferred_element_type=F32)
              + jnp.dot(ya_ref[rs, :], wo_ref[d_ssm:, :], preferred_element_type=F32))
        x3 = _swiglu_residual(x2, g2_ref, wgu_ref, wd_ref, fc)
        o_ref[rs, :] = _rms(x3, gf_ref[...])


def _const_spec(shape):
    nd = len(shape)
    return pl.BlockSpec(shape, lambda *_: (0,) * nd, pipeline_mode=pl.Buffered(1))


def _rope_lane_tables(seq):
    half = ROPE_DIM // 2
    inv_freq = ROPE_THETA ** (-jnp.arange(half, dtype=F32) * 2.0 / ROPE_DIM)
    ang = jnp.arange(seq, dtype=F32)[:, None] * inv_freq[None, :]
    cos, sin = jnp.cos(ang), jnp.sin(ang)
    ones = jnp.ones((seq, HEAD_DIM - ROPE_DIM), F32)
    zeros = jnp.zeros((seq, HEAD_DIM - ROPE_DIM), F32)
    zh = jnp.zeros((seq, half), F32)
    c = jnp.concatenate([cos, cos, ones], axis=-1)
    s1 = jnp.concatenate([zh, sin, zeros], axis=-1)
    s2 = jnp.concatenate([-sin, zh, zeros], axis=-1)
    reps = LANES // HEAD_DIM
    return tuple(jnp.tile(t, (1, reps)) for t in (c, s1, s2))


def _gate_up(wg, wu, fc):
    d, f = wg.shape
    pair = jnp.stack([wg.reshape(d, f // fc, fc), wu.reshape(d, f // fc, fc)], axis=2)
    return pair.reshape(d, 2 * f).astype(BF16)


def _block_diag(m, n_halves):
    g, r, c = m.shape
    gh = g // n_halves
    m = m.reshape(n_halves, gh, r, c)
    eye = jnp.eye(gh, dtype=m.dtype)
    return jnp.einsum('hgrc,gk->hgrkc', m, eye).reshape(n_halves, gh * r, gh * c)


def _ssm_params(a_re, a_im, log_dt, b_re, b_im, c_re, c_im, n_chain):
    lam = lax.complex(a_re, a_im)
    dt = jnp.exp(log_dt)[:, None]
    lam_bar = jnp.exp(lam * dt)
    zoh = (lam_bar - 1.0) / lam
    b_bar = zoh[..., None] * lax.complex(b_re, b_im)
    n_state = lam_bar.size
    are = jnp.broadcast_to(lam_bar.real.reshape(1, n_state), (SUBLANES, n_state))
    aim = jnp.broadcast_to(lam_bar.imag.reshape(1, n_state), (SUBLANES, n_state))
    bre = _block_diag(jnp.swapaxes(b_bar.real, 1, 2), n_chain).astype(BF16)
    bim = _block_diag(jnp.swapaxes(b_bar.imag, 1, 2), n_chain).astype(BF16)
    cre = _block_diag(jnp.swapaxes(c_re, 1, 2), n_chain).astype(BF16)
    cim = _block_diag(jnp.swapaxes(-c_im, 1, 2), n_chain).astype(BF16)
    return are, aim, bre, bim, cre, cim


def _layer(x, cs, ffn1_norm, wg1, wu1, wd1, mix_norm, w_in, ssm_A_re, ssm_A_im, ssm_log_dt,
           ssm_B_re, ssm_B_im, ssm_C_re, ssm_C_im, ssm_D, ssm_w_glu, ssm_b_glu, attn_sinks,
           ssm_out_norm, attn_out_norm, w_out, ffn2_norm, wg2, wu2, wd2, final_norm):
    bsz, seq, d_model = x.shape
    d_ff = wg1.shape[1]
    d_ssm = ssm_D.shape[0]
    d_attn = attn_out_norm.shape[0]
    d_kv = N_KV_HEADS * HEAD_DIM
    d_in = w_in.shape[1]
    tm = 512
    sub = 512
    fc = MXU_N
    tt = 64
    n_chain = 4
    nq = 4
    row = lambda v: v.reshape(1, -1).astype(F32)

    col_scale = jnp.concatenate([jnp.ones((d_ssm,), F32),
                                 jnp.full((d_attn,), LOG2E / math.sqrt(HEAD_DIM), F32),
                                 jnp.ones((2 * d_kv,), F32)])
    w_in_b = (w_in * col_scale[None, :]).astype(BF16)

    cparams = lambda sem: pltpu.CompilerParams(dimension_semantics=sem, vmem_limit_bytes=VMEM_LIMIT)

    tile = lambda w: pl.BlockSpec((None, tm, w), lambda b, i: (b, i, 0))
    rope_spec = pl.BlockSpec((tm, LANES), lambda b, i: (i, 0))
    tm_tile = pl.BlockSpec((tm, d_ssm), lambda b, i: (i, b))
    x1, u_tb, q, k, v = pl.pallas_call(
        functools.partial(_ffn_in_kernel, fc=fc, sub=sub, d_ssm=d_ssm, d_attn=d_attn, d_kv=d_kv),
        grid=(bsz, seq // tm),
        in_specs=[tile(d_model), _const_spec((1, d_model)), _const_spec((d_model, 2 * d_ff)),
                  _const_spec((d_ff, d_model)),
                  _const_spec((1, d_model)), _const_spec((d_model, d_in)),
                  rope_spec, rope_spec, rope_spec],
        out_specs=[tile(d_model),
                   tm_tile,
                   tile(d_attn), tile(d_kv), tile(d_kv)],
        out_shape=[jax.ShapeDtypeStruct((bsz, seq, d_model), F32),
                   jax.ShapeDtypeStruct((seq, bsz * d_ssm), F32),
                   jax.ShapeDtypeStruct((bsz, seq, d_attn), BF16),
                   jax.ShapeDtypeStruct((bsz, seq, d_kv), BF16),
                   jax.ShapeDtypeStruct((bsz, seq, d_kv), BF16)],
        compiler_params=cparams(("parallel", "parallel")),
        name="ffn_in",
    )(x, row(ffn1_norm), _gate_up(wg1, wu1, fc), wd1.astype(BF16),
      row(mix_norm), w_in_b, *cs)

    assert bsz == SUBLANES
    are, aim, bre, bim, cre, cim = _ssm_params(ssm_A_re, ssm_A_im, ssm_log_dt,
                                               ssm_B_re, ssm_B_im, ssm_C_re, ssm_C_im, n_chain)
    n_state = are.shape[-1]
    tb_spec = pl.BlockSpec((tt, bsz, d_ssm), lambda i: (i, 0, 0))
    y_ssm_tb = pl.pallas_call(
        functools.partial(_ssm_kernel, tt=tt, n_chain=n_chain),
        grid=(seq // tt,),
        in_specs=[tb_spec, _const_spec(are.shape), _const_spec(aim.shape),
                  _const_spec(bre.shape), _const_spec(bim.shape),
                  _const_spec(cre.shape), _const_spec(cim.shape),
                  _const_spec((1, d_ssm)), _const_spec((d_ssm, d_ssm)),
                  _const_spec((1, d_ssm)), _const_spec((1, d_ssm))],
        out_specs=tb_spec,
        out_shape=jax.ShapeDtypeStruct((seq, bsz, d_ssm), F32),
        scratch_shapes=([pltpu.VMEM((bsz, n_state), F32)] * 2
                        + [pltpu.VMEM((tt * bsz, n_state // n_chain), F32)] * (2 * n_chain)),
        compiler_params=cparams(("arbitrary",)),
        name="ssm",
    )(u_tb.reshape(seq, bsz, d_ssm), are, aim, bre, bim, cre, cim,
      row(ssm_D), ssm_w_glu.astype(BF16), row(ssm_b_glu), row(ssm_out_norm))

    tq = nq * WINDOW
    cur = lambda w: pl.BlockSpec((None, tq, w), lambda b, n: (b, n, 0))
    prev = lambda w: pl.BlockSpec((None, WINDOW, w), lambda b, n: (b, jnp.maximum(n * nq - 1, 0), 0))
    y_attn = pl.pallas_call(
        functools.partial(_attn_kernel, nq=nq),
        grid=(bsz, seq // tq),
        in_specs=[pl.BlockSpec(memory_space=pltpu.SMEM),
                  cur(d_attn), cur(d_kv), prev(d_kv), cur(d_kv), prev(d_kv),
                  pl.BlockSpec((1, d_attn), lambda b, n: (0, 0))],
        out_specs=cur(d_attn),
        out_shape=jax.ShapeDtypeStruct((bsz, seq, d_attn), BF16),
        compiler_params=cparams(("parallel", "parallel")),
        name="attn",
    )(attn_sinks.astype(F32), q, k, k, v, v, row(attn_out_norm))

    out = pl.pallas_call(
        functools.partial(_out_ffn_kernel, fc=fc, sub=sub),
        grid=(bsz, seq // tm),
        in_specs=[tile(d_model),
                  tm_tile,
                  tile(d_attn), _const_spec((d_ssm + d_attn, d_model)),
                  _const_spec((1, d_model)), _const_spec((d_model, 2 * d_ff)),
                  _const_spec((d_ff, d_model)),
                  _const_spec((1, d_model))],
        out_specs=tile(d_model),
        out_shape=jax.ShapeDtypeStruct((bsz, seq, d_model), F32),
        compiler_params=cparams(("parallel", "parallel")),
        name="out_ffn",
    )(x1, y_ssm_tb.reshape(seq, bsz * d_ssm), y_attn, w_out.astype(BF16),
      row(ffn2_norm), _gate_up(wg2, wu2, fc), wd2.astype(BF16), row(final_norm))
    return out


def kernel(x, ffn1_norm, ffn1_w_gate, ffn1_w_up, ffn1_w_down, mix_norm, w_in, ssm_A_re, ssm_A_im, ssm_log_dt, ssm_B_re, ssm_B_im, ssm_C_re, ssm_C_im, ssm_D, ssm_w_glu, ssm_b_glu, attn_sinks, ssm_out_norm, attn_out_norm, w_out, ffn2_norm, ffn2_w_gate, ffn2_w_up, ffn2_w_down, final_norm):
    depth = ffn1_norm.shape[0]
    assert depth == 1, "the final norm is fused into the last layer's kernel"
    cs = _rope_lane_tables(x.shape[1])
    l = 0
    return _layer(x, cs, ffn1_norm[l], ffn1_w_gate[l], ffn1_w_up[l], ffn1_w_down[l], mix_norm[l],
                  w_in[l], ssm_A_re[l], ssm_A_im[l], ssm_log_dt[l], ssm_B_re[l], ssm_B_im[l],
                  ssm_C_re[l], ssm_C_im[l], ssm_D[l], ssm_w_glu[l], ssm_b_glu[l], attn_sinks[l],
                  ssm_out_norm[l], attn_out_norm[l], w_out[l], ffn2_norm[l], ffn2_w_gate[l],
                  ffn2_w_up[l], ffn2_w_down[l], final_norm)
```

```python
import functools
import math

import jax
import jax.numpy as jnp
from jax import lax
from jax.experimental import pallas as pl
from jax.experimental.pallas import tpu as pltpu

F32 = jnp.float32
BF16 = jnp.bfloat16

SSM_GROUP = 16
SSM_STATE = 64
HEAD_DIM = 64
N_KV_HEADS = 2
WINDOW = 128
ROPE_DIM = HEAD_DIM // 4
ROPE_THETA = 500000.0
RES_HALF = 0.5
EPS = 1e-6
NEG_INF = -1e30
LOG2E = math.log2(math.e)
LANES = 128
SUBLANES = 8
MXU_N = 256

VMEM_LIMIT = 56 * 1024 * 1024


def _rms(x, g):
    return x * lax.rsqrt(jnp.mean(x * x, axis=-1, keepdims=True) + EPS) * g


def _swiglu_residual(x, g_ref, wg_ref, wu_ref, wd_ref, fc):
    hn = _rms(x, g_ref[...]).astype(BF16)
    acc = x
    for c in range(wd_ref.shape[0] // fc):
        sl = slice(c * fc, (c + 1) * fc)
        g = jnp.dot(hn, wg_ref[:, sl], preferred_element_type=F32)
        u = jnp.dot(hn, wu_ref[:, sl], preferred_element_type=F32)
        a = (g * jax.nn.sigmoid(g) * u).astype(BF16)
        acc = acc + RES_HALF * jnp.dot(a, wd_ref[sl, :], preferred_element_type=F32)
    return acc


def _rope(t, c, s1, s2):
    half = ROPE_DIM // 2
    return t * c + pltpu.roll(t, half, 1) * s1 + pltpu.roll(t, LANES - half, 1) * s2


def _ffn_in_kernel(x_ref, g1_ref, wg_ref, wu_ref, wd_ref, gm_ref, win_ref,
                   rc_ref, rs1_ref, rs2_ref,
                   x1_ref, u_ref, q_ref, k_ref, v_ref, *, fc, d_ssm, d_attn, d_kv):
    x1 = _swiglu_residual(x_ref[...], g1_ref, wg_ref, wu_ref, wd_ref, fc)
    x1_ref[...] = x1
    hn = _rms(x1, gm_ref[...]).astype(BF16)
    u_ref[...] = jnp.dot(hn, win_ref[:, :d_ssm], preferred_element_type=F32)
    c, s1, s2 = rc_ref[...], rs1_ref[...], rs2_ref[...]
    qkv = jnp.dot(hn, win_ref[:, d_ssm:], preferred_element_type=F32)
    for j in range(d_attn // LANES):
        sl = slice(j * LANES, (j + 1) * LANES)
        q_ref[:, sl] = _rope(qkv[:, sl], c, s1, s2).astype(BF16)
    k_ref[...] = _rope(qkv[:, d_attn:d_attn + d_kv], c, s1, s2).astype(BF16)
    v_ref[...] = qkv[:, d_attn + d_kv:].astype(BF16)


def _ssm_kernel(u_ref, are_ref, aim_ref, bre_ref, bim_ref, cre_ref, cim_ref,
                d_ref, wglu_ref, bglu_ref, gn_ref,
                y_ref, xre_ref, xim_ref, *s_refs, tt, n_chain):
    sre, sim = s_refs[:n_chain], s_refs[n_chain:]
    rows = tt * SUBLANES
    d_ssm = u_ref.shape[-1]
    cw = d_ssm // n_chain
    sw = are_ref.shape[-1] // n_chain

    @pl.when(pl.program_id(0) == 0)
    def _():
        xre_ref[...] = jnp.zeros_like(xre_ref)
        xim_ref[...] = jnp.zeros_like(xim_ref)

    u = u_ref[...].reshape(rows, d_ssm)
    ub = u.astype(BF16)
    for c in range(n_chain):
        uc = ub[:, c * cw:(c + 1) * cw]
        sre[c][...] = jnp.dot(uc, bre_ref[c], preferred_element_type=F32)
        sim[c][...] = jnp.dot(uc, bim_ref[c], preferred_element_type=F32)

    ys = []
    for c in range(n_chain):
        st = slice(c * sw, (c + 1) * sw)
        ar, ai = are_ref[:, st], aim_ref[:, st]
        xr, xi = xre_ref[:, st], xim_ref[:, st]
        for t in range(tt):
            r = slice(t * SUBLANES, (t + 1) * SUBLANES)
            xr, xi = (ar * xr - ai * xi + sre[c][r, :],
                      ar * xi + ai * xr + sim[c][r, :])
            sre[c][r, :] = xr
            sim[c][r, :] = xi
        xre_ref[:, st] = xr
        xim_ref[:, st] = xi
        ys.append(jnp.dot(sre[c][...].astype(BF16), cre_ref[c], preferred_element_type=F32)
                  + jnp.dot(sim[c][...].astype(BF16), cim_ref[c], preferred_element_type=F32))
    y = jnp.concatenate(ys, axis=-1) + d_ref[...] * u
    y = jax.nn.gelu(y)
    z = jnp.dot(y.astype(BF16), wglu_ref[...], preferred_element_type=F32) + bglu_ref[...]
    y = y * jax.nn.sigmoid(z)
    y_ref[...] = _rms(y, gn_ref[...]).reshape(tt, SUBLANES, d_ssm)


def _attn_kernel(sink_ref, q_ref, kc_ref, kp_ref, vc_ref, vp_ref, gn_ref, o_ref, *, nq):
    blk = WINDOW
    n = pl.program_id(1)
    lane = lax.broadcasted_iota(jnp.int32, (1, LANES), 1)
    k_all = jnp.concatenate([kp_ref[...], kc_ref[...]], axis=0).astype(F32)
    v_all = jnp.concatenate([vp_ref[...], vc_ref[...]], axis=0).astype(F32)

    def placed(t, kh, half):
        t = jnp.where((lane // HEAD_DIM) == kh, t, 0.0)
        if half != kh:
            t = pltpu.roll(t, HEAD_DIM, 1)
        return t

    bf_tile = 2 * SUBLANES
    row0 = lax.broadcasted_iota(jnp.int32, (bf_tile, 1), 0) == 0
    kmat, vext, vhead = [], [], []
    for kh in range(N_KV_HEADS):
        kmat.append([]), vext.append([]), vhead.append([])
        for half in range(2):
            ones = jnp.broadcast_to(jnp.where((lane // HEAD_DIM) == half, 1.0, 0.0), v_all.shape)
            vp = placed(v_all, kh, half)
            kmat[kh].append(placed(k_all, kh, half).astype(BF16))
            vext[kh].append(jnp.concatenate([vp, ones], axis=-1).astype(BF16))
            vhead[kh].append([
                jnp.concatenate([jnp.where(row0, 0.0, vp[j * blk:j * blk + bf_tile]),
                                 ones[:bf_tile]], axis=-1).astype(BF16)
                for j in range(nq)])

    qi = lax.broadcasted_iota(jnp.int32, (blk, 2 * blk), 0)
    sj = lax.broadcasted_iota(jnp.int32, (blk, 2 * blk), 1)
    band = (sj > qi) & (sj <= qi + blk)
    band_first = band & ((sj >= blk) | (n > 0))
    col0 = sj == 0
    n_pairs = q_ref.shape[-1] // LANES
    heads_per_kv = (2 * n_pairs) // N_KV_HEADS

    outs = []
    for j in range(nq):
        valid = band_first if j == 0 else band
        pair_outs = []
        for pr in range(n_pairs):
            qp = q_ref[j * blk:(j + 1) * blk, pr * LANES:(pr + 1) * LANES]
            res = None
            for half in range(2):
                h = 2 * pr + half
                kh = h // heads_per_kv
                fill = jnp.where(col0, sink_ref[h] * LOG2E, NEG_INF)
                s = lax.dot_general(qp, kmat[kh][half][j * blk:(j + 2) * blk],
                                    (((1,), (1,)), ((), ())), preferred_element_type=F32)
                s = jnp.where(valid, s, fill)
                m = jnp.max(s, axis=-1, keepdims=True)
                p = jnp.exp2(s - m).astype(BF16)
                win = jnp.concatenate([vhead[kh][half][j],
                                       vext[kh][half][j * blk + bf_tile:(j + 2) * blk]], axis=0)
                o = jnp.dot(p, win, preferred_element_type=F32)
                res = o if res is None else res + o
            pair_outs.append(res[:, :LANES] / res[:, LANES:])
        outs.append(jnp.concatenate(pair_outs, axis=-1))
    y = jnp.concatenate(outs, axis=0)
    o_ref[...] = _rms(y, gn_ref[...]).astype(o_ref.dtype)


def _out_ffn_kernel(x1_ref, ys_ref, ya_ref, wo_ref, g2_ref, wg_ref, wu_ref, wd_ref, gf_ref,
                    o_ref, *, fc):
    d_ssm = ys_ref.shape[-1]
    x2 = (x1_ref[...]
          + jnp.dot(ys_ref[...].astype(BF16), wo_ref[:d_ssm, :], preferred_element_type=F32)
          + jnp.dot(ya_ref[...], wo_ref[d_ssm:, :], preferred_element_type=F32))
    x3 = _swiglu_residual(x2, g2_ref, wg_ref, wu_ref, wd_ref, fc)
    o_ref[...] = _rms(x3, gf_ref[...])


def _const_spec(shape):
    nd = len(shape)
    return pl.BlockSpec(shape, lambda *_: (0,) * nd, pipeline_mode=pl.Buffered(1))


def _rope_lane_tables(seq):
    half = ROPE_DIM // 2
    inv_freq = ROPE_THETA ** (-jnp.arange(half, dtype=F32) * 2.0 / ROPE_DIM)
    ang = jnp.arange(seq, dtype=F32)[:, None] * inv_freq[None, :]
    cos, sin = jnp.cos(ang), jnp.sin(ang)
    ones = jnp.ones((seq, HEAD_DIM - ROPE_DIM), F32)
    zeros = jnp.zeros((seq, HEAD_DIM - ROPE_DIM), F32)
    zh = jnp.zeros((seq, half), F32)
    c = jnp.concatenate([cos, cos, ones], axis=-1)
    s1 = jnp.concatenate([zh, sin, zeros], axis=-1)
    s2 = jnp.concatenate([-sin, zh, zeros], axis=-1)
    reps = LANES // HEAD_DIM
    return tuple(jnp.tile(t, (1, reps)) for t in (c, s1, s2))


def _block_diag(m, n_halves):
    g, r, c = m.shape
    gh = g // n_halves
    m = m.reshape(n_halves, gh, r, c)
    eye = jnp.eye(gh, dtype=m.dtype)
    return jnp.einsum('hgrc,gk->hgrkc', m, eye).reshape(n_halves, gh * r, gh * c)


def _ssm_params(a_re, a_im, log_dt, b_re, b_im, c_re, c_im, n_chain):
    lam = lax.complex(a_re, a_im)
    dt = jnp.exp(log_dt)[:, None]
    lam_bar = jnp.exp(lam * dt)
    zoh = (lam_bar - 1.0) / lam
    b_bar = zoh[..., None] * lax.complex(b_re, b_im)
    n_state = lam_bar.size
    are = jnp.broadcast_to(lam_bar.real.reshape(1, n_state), (SUBLANES, n_state))
    aim = jnp.broadcast_to(lam_bar.imag.reshape(1, n_state), (SUBLANES, n_state))
    bre = _block_diag(jnp.swapaxes(b_bar.real, 1, 2), n_chain).astype(BF16)
    bim = _block_diag(jnp.swapaxes(b_bar.imag, 1, 2), n_chain).astype(BF16)
    cre = _block_diag(jnp.swapaxes(c_re, 1, 2), n_chain).astype(BF16)
    cim = _block_diag(jnp.swapaxes(-c_im, 1, 2), n_chain).astype(BF16)
    return are, aim, bre, bim, cre, cim


def _layer(x, cs, ffn1_norm, wg1, wu1, wd1, mix_norm, w_in, ssm_A_re, ssm_A_im, ssm_log_dt,
           ssm_B_re, ssm_B_im, ssm_C_re, ssm_C_im, ssm_D, ssm_w_glu, ssm_b_glu, attn_sinks,
           ssm_out_norm, attn_out_norm, w_out, ffn2_norm, wg2, wu2, wd2, final_norm):
    bsz, seq, d_model = x.shape
    d_ff = wg1.shape[1]
    d_ssm = ssm_D.shape[0]
    d_attn = attn_out_norm.shape[0]
    d_kv = N_KV_HEADS * HEAD_DIM
    d_in = w_in.shape[1]
    tm = 1024
    fc = MXU_N
    tt = 64
    n_chain = 4
    nq = 4
    row = lambda v: v.reshape(1, -1).astype(F32)

    col_scale = jnp.concatenate([jnp.ones((d_ssm,), F32),
                                 jnp.full((d_attn,), LOG2E / math.sqrt(HEAD_DIM), F32),
                                 jnp.ones((2 * d_kv,), F32)])
    w_in_b = (w_in * col_scale[None, :]).astype(BF16)

    cparams = lambda sem: pltpu.CompilerParams(dimension_semantics=sem, vmem_limit_bytes=VMEM_LIMIT)

    tile = lambda w: pl.BlockSpec((None, tm, w), lambda b, i: (b, i, 0))
    rope_spec = pl.BlockSpec((tm, LANES), lambda b, i: (i, 0))
    tm_tile = pl.BlockSpec((tm, d_ssm), lambda b, i: (i, b))
    x1, u_tb, q, k, v = pl.pallas_call(
        functools.partial(_ffn_in_kernel, fc=fc, d_ssm=d_ssm, d_attn=d_attn, d_kv=d_kv),
        grid=(bsz, seq // tm),
        in_specs=[tile(d_model), _const_spec((1, d_model)), _const_spec((d_model, d_ff)),
                  _const_spec((d_model, d_ff)), _const_spec((d_ff, d_model)),
                  _const_spec((1, d_model)), _const_spec((d_model, d_in)),
                  rope_spec, rope_spec, rope_spec],
        out_specs=[tile(d_model),
                   tm_tile,
                   tile(d_attn), tile(d_kv), tile(d_kv)],
        out_shape=[jax.ShapeDtypeStruct((bsz, seq, d_model), F32),
                   jax.ShapeDtypeStruct((seq, bsz * d_ssm), F32),
                   jax.ShapeDtypeStruct((bsz, seq, d_attn), BF16),
                   jax.ShapeDtypeStruct((bsz, seq, d_kv), BF16),
                   jax.ShapeDtypeStruct((bsz, seq, d_kv), BF16)],
        compiler_params=cparams(("parallel", "parallel")),
        name="ffn_in",
    )(x, row(ffn1_norm), wg1.astype(BF16), wu1.astype(BF16), wd1.astype(BF16),
      row(mix_norm), w_in_b, *cs)

    assert bsz == SUBLANES
    are, aim, bre, bim, cre, cim = _ssm_params(ssm_A_re, ssm_A_im, ssm_log_dt,
                                               ssm_B_re, ssm_B_im, ssm_C_re, ssm_C_im, n_chain)
    n_state = are.shape[-1]
    tb_spec = pl.BlockSpec((tt, bsz, d_ssm), lambda i: (i, 0, 0))
    y_ssm_tb = pl.pallas_call(
        functools.partial(_ssm_kernel, tt=tt, n_chain=n_chain),
        grid=(seq // tt,),
        in_specs=[tb_spec, _const_spec(are.shape), _const_spec(aim.shape),
                  _const_spec(bre.shape), _const_spec(bim.shape),
                  _const_spec(cre.shape), _const_spec(cim.shape),
                  _const_spec((1, d_ssm)), _const_spec((d_ssm, d_ssm)),
                  _const_spec((1, d_ssm)), _const_spec((1, d_ssm))],
        out_specs=tb_spec,
        out_shape=jax.ShapeDtypeStruct((seq, bsz, d_ssm), F32),
        scratch_shapes=([pltpu.VMEM((bsz, n_state), F32)] * 2
                        + [pltpu.VMEM((tt * bsz, n_state // n_chain), F32)] * (2 * n_chain)),
        compiler_params=cparams(("arbitrary",)),
        name="ssm",
    )(u_tb.reshape(seq, bsz, d_ssm), are, aim, bre, bim, cre, cim,
      row(ssm_D), ssm_w_glu.astype(BF16), row(ssm_b_glu), row(ssm_out_norm))

    tq = nq * WINDOW
    cur = lambda w: pl.BlockSpec((None, tq, w), lambda b, n: (b, n, 0))
    prev = lambda w: pl.BlockSpec((None, WINDOW, w), lambda b, n: (b, jnp.maximum(n * nq - 1, 0), 0))
    y_attn = pl.pallas_call(
        functools.partial(_attn_kernel, nq=nq),
        grid=(bsz, seq // tq),
        in_specs=[pl.BlockSpec(memory_space=pltpu.SMEM),
                  cur(d_attn), cur(d_kv), prev(d_kv), cur(d_kv), prev(d_kv),
                  pl.BlockSpec((1, d_attn), lambda b, n: (0, 0))],
        out_specs=cur(d_attn),
        out_shape=jax.ShapeDtypeStruct((bsz, seq, d_attn), BF16),
        compiler_params=cparams(("parallel", "parallel")),
        name="attn",
    )(attn_sinks.astype(F32), q, k, k, v, v, row(attn_out_norm))

    out = pl.pallas_call(
        functools.partial(_out_ffn_kernel, fc=fc),
        grid=(bsz, seq // tm),
        in_specs=[tile(d_model),
                  tm_tile,
                  tile(d_attn), _const_spec((d_ssm + d_attn, d_model)),
                  _const_spec((1, d_model)), _const_spec((d_model, d_ff)),
                  _const_spec((d_model, d_ff)), _const_spec((d_ff, d_model)),
                  _const_spec((1, d_model))],
        out_specs=tile(d_model),
        out_shape=jax.ShapeDtypeStruct((bsz, seq, d_model), F32),
        compiler_params=cparams(("parallel", "parallel")),
        name="out_ffn",
    )(x1, y_ssm_tb.reshape(seq, bsz * d_ssm), y_attn, w_out.astype(BF16),
      row(ffn2_norm), wg2.astype(BF16), wu2.astype(BF16), wd2.astype(BF16), row(final_norm))
    return out


def kernel(x, ffn1_norm, ffn1_w_gate, ffn1_w_up, ffn1_w_down, mix_norm, w_in, ssm_A_re, ssm_A_im, ssm_log_dt, ssm_B_re, ssm_B_im, ssm_C_re, ssm_C_im, ssm_D, ssm_w_glu, ssm_b_glu, attn_sinks, ssm_out_norm, attn_out_norm, w_out, ffn2_norm, ffn2_w_gate, ffn2_w_up, ffn2_w_down, final_norm):
    depth = ffn1_norm.shape[0]
    assert depth == 1, "the final norm is fused into the last layer's kernel"
    cs = _rope_lane_tables(x.shape[1])
    l = 0
    return _layer(x, cs, ffn1_norm[l], ffn1_w_gate[l], ffn1_w_up[l], ffn1_w_down[l], mix_norm[l],
                  w_in[l], ssm_A_re[l], ssm_A_im[l], ssm_log_dt[l], ssm_B_re[l], ssm_B_im[l],
                  ssm_C_re[l], ssm_C_im[l], ssm_D[l], ssm_w_glu[l], ssm_b_glu[l], attn_sinks[l],
                  ssm_out_norm[l], attn_out_norm[l], w_out[l], ffn2_norm[l], ffn2_w_gate[l],
                  ffn2_w_up[l], ffn2_w_down[l], final_norm)
```

```python
import functools
import math

import jax
import jax.numpy as jnp
from jax import lax
from jax.experimental import pallas as pl
from jax.experimental.pallas import tpu as pltpu

F32 = jnp.float32
BF16 = jnp.bfloat16

SSM_GROUP = 16
SSM_STATE = 64
HEAD_DIM = 64
N_KV_HEADS = 2
WINDOW = 128
ROPE_DIM = HEAD_DIM // 4
ROPE_THETA = 500000.0
RES_HALF = 0.5
EPS = 1e-6
NEG_INF = -1e30
LOG2E = math.log2(math.e)
LANES = 128
SUBLANES = 8
MXU_N = 256

VMEM_LIMIT = 56 * 1024 * 1024


def _rms(x, g):
    return x * lax.rsqrt(jnp.mean(x * x, axis=-1, keepdims=True) + EPS) * g


def _swiglu_residual(x, g_ref, wg_ref, wu_ref, wd_ref, fc):
    hn = _rms(x, g_ref[...]).astype(BF16)
    acc = x
    for c in range(wd_ref.shape[0] // fc):
        sl = slice(c * fc, (c + 1) * fc)
        g = jnp.dot(hn, wg_ref[:, sl], preferred_element_type=F32)
        u = jnp.dot(hn, wu_ref[:, sl], preferred_element_type=F32)
        a = (g * jax.nn.sigmoid(g) * u).astype(BF16)
        acc = acc + RES_HALF * jnp.dot(a, wd_ref[sl, :], preferred_element_type=F32)
    return acc


def _rope(t, c, s1, s2):
    half = ROPE_DIM // 2
    return t * c + pltpu.roll(t, half, 1) * s1 + pltpu.roll(t, LANES - half, 1) * s2


def _placed_kv(t):
    lane = lax.broadcasted_iota(jnp.int32, (1, LANES), 1)
    head0 = jnp.where(lane < HEAD_DIM, t, 0.0)
    head1 = jnp.where(lane < HEAD_DIM, 0.0, t)
    blocks = [head0, pltpu.roll(head0, HEAD_DIM, 1), pltpu.roll(head1, HEAD_DIM, 1), head1]
    return jnp.concatenate(blocks, axis=-1).astype(BF16)


def _ffn_in_kernel(x_ref, g1_ref, wg_ref, wu_ref, wd_ref, gm_ref, win_ref,
                   rc_ref, rs1_ref, rs2_ref,
                   x1_ref, u_ref, q_ref, k_ref, v_ref, *, fc, d_ssm, d_attn, d_kv):
    x1 = _swiglu_residual(x_ref[...], g1_ref, wg_ref, wu_ref, wd_ref, fc)
    x1_ref[...] = x1
    hn = _rms(x1, gm_ref[...]).astype(BF16)
    u_ref[...] = jnp.dot(hn, win_ref[:, :d_ssm], preferred_element_type=F32)
    c, s1, s2 = rc_ref[...], rs1_ref[...], rs2_ref[...]
    qkv = jnp.dot(hn, win_ref[:, d_ssm:], preferred_element_type=F32)
    for j in range(d_attn // LANES):
        sl = slice(j * LANES, (j + 1) * LANES)
        q_ref[:, sl] = _rope(qkv[:, sl], c, s1, s2).astype(BF16)
    k_ref[...] = _placed_kv(_rope(qkv[:, d_attn:d_attn + d_kv], c, s1, s2))
    v_ref[...] = _placed_kv(qkv[:, d_attn + d_kv:])


def _ssm_kernel(u_ref, uold_ref, are_ref, aim_ref, bre_ref, bim_ref, cre_ref, cim_ref,
                d_ref, wglu_ref, bglu_ref, gn_ref,
                y_ref, xre_ref, xim_ref, *bufs, tt, n_chain):
    rows = tt * SUBLANES
    d_ssm = u_ref.shape[-1]
    cw = d_ssm // n_chain
    sw = are_ref.shape[-1] // n_chain
    buf = [[bufs[(par * 4 + kind) * n_chain:(par * 4 + kind + 1) * n_chain] for kind in range(4)]
           for par in range(2)]

    @pl.when(pl.program_id(0) == 0)
    def _():
        xre_ref[...] = jnp.zeros_like(xre_ref)
        xim_ref[...] = jnp.zeros_like(xim_ref)
        for c in range(n_chain):
            buf[1][0][c][...] = jnp.zeros((rows, sw), F32)
            buf[1][1][c][...] = jnp.zeros((rows, sw), F32)
            buf[0][2][c][...] = jnp.zeros((rows, sw), BF16)
            buf[0][3][c][...] = jnp.zeros((rows, sw), BF16)

    for par in range(2):
        tr = slice(par * tt, (par + 1) * tt)
        bu_re, bu_im, st_re, st_im = buf[par]
        sc_bre, sc_bim, sc_sre, sc_sim = buf[1 - par]

        ub = u_ref[tr].reshape(rows, d_ssm).astype(BF16)
        ys = []
        for c in range(n_chain):
            st = slice(c * sw, (c + 1) * sw)
            ar, ai = are_ref[:, st], aim_ref[:, st]
            xr, xi = xre_ref[:, st], xim_ref[:, st]
            for t in range(0, tt, 2):
                pair_re, pair_im = [], []
                for tp in (t, t + 1):
                    r = slice(tp * SUBLANES, (tp + 1) * SUBLANES)
                    xr, xi = (ar * xr - ai * xi + sc_bre[c][r, :],
                              ar * xi + ai * xr + sc_bim[c][r, :])
                    pair_re.append(xr)
                    pair_im.append(xi)
                r2 = slice(t * SUBLANES, (t + 2) * SUBLANES)
                sc_sre[c][r2, :] = jnp.concatenate(pair_re, axis=0).astype(BF16)
                sc_sim[c][r2, :] = jnp.concatenate(pair_im, axis=0).astype(BF16)
            xre_ref[:, st] = xr
            xim_ref[:, st] = xi

            uc = ub[:, c * cw:(c + 1) * cw]
            bu_re[c][...] = jnp.dot(uc, bre_ref[c], preferred_element_type=F32)
            bu_im[c][...] = jnp.dot(uc, bim_ref[c], preferred_element_type=F32)

            ys.append(jnp.dot(st_re[c][...], cre_ref[c], preferred_element_type=F32)
                      + jnp.dot(st_im[c][...], cim_ref[c], preferred_element_type=F32))
        y = jnp.concatenate(ys, axis=-1) + d_ref[...] * uold_ref[tr].reshape(rows, d_ssm)
        y = jax.nn.gelu(y)
        z = jnp.dot(y.astype(BF16), wglu_ref[...], preferred_element_type=F32) + bglu_ref[...]
        y = y * jax.nn.sigmoid(z)
        y_ref[tr] = _rms(y, gn_ref[...]).reshape(tt, SUBLANES, d_ssm)


def _attn_kernel(sink_ref, q_ref, kc_ref, kp_ref, vc_ref, vp_ref, gn_ref, o_ref, *, nq):
    blk = WINDOW
    n = pl.program_id(1)
    lane = lax.broadcasted_iota(jnp.int32, (1, LANES), 1)
    k_all = jnp.concatenate([kp_ref[...], kc_ref[...]], axis=0)
    v_all = jnp.concatenate([vp_ref[...], vc_ref[...]], axis=0)

    bf_tile = 2 * SUBLANES
    row0 = lax.broadcasted_iota(jnp.int32, (bf_tile, 1), 0) == 0
    kmat, vext, vhead = [], [], []
    for kh in range(N_KV_HEADS):
        kmat.append([]), vext.append([]), vhead.append([])
        for half in range(2):
            sl = slice((kh * 2 + half) * LANES, (kh * 2 + half + 1) * LANES)
            ones = jnp.broadcast_to(jnp.where((lane // HEAD_DIM) == half, 1.0, 0.0),
                                    (v_all.shape[0], LANES)).astype(BF16)
            kmat[kh].append(k_all[:, sl])
            vext[kh].append(jnp.concatenate([v_all[:, sl], ones], axis=-1))
            vhead[kh].append([
                jnp.concatenate(
                    [jnp.where(row0, 0.0, v_all[j * blk:j * blk + bf_tile, sl].astype(F32)).astype(BF16),
                     ones[:bf_tile]], axis=-1)
                for j in range(nq)])

    qi = lax.broadcasted_iota(jnp.int32, (blk, 2 * blk), 0)
    sj = lax.broadcasted_iota(jnp.int32, (blk, 2 * blk), 1)
    band = (sj > qi) & (sj <= qi + blk)
    band_first = band & ((sj >= blk) | (n > 0))
    col0 = sj == 0
    n_pairs = q_ref.shape[-1] // LANES
    heads_per_kv = (2 * n_pairs) // N_KV_HEADS

    outs = []
    for j in range(nq):
        valid = band_first if j == 0 else band
        pair_outs = []
        for pr in range(n_pairs):
            qp = q_ref[j * blk:(j + 1) * blk, pr * LANES:(pr + 1) * LANES]
            res = None
            for half in range(2):
                h = 2 * pr + half
                kh = h // heads_per_kv
                fill = jnp.where(col0, sink_ref[h] * LOG2E, NEG_INF)
                s = lax.dot_general(qp, kmat[kh][half][j * blk:(j + 2) * blk],
                                    (((1,), (1,)), ((), ())), preferred_element_type=F32)
                s = jnp.where(valid, s, fill)
                m = jnp.max(s, axis=-1, keepdims=True)
                p = jnp.exp2(s - m).astype(BF16)
                win = jnp.concatenate([vhead[kh][half][j],
                                       vext[kh][half][j * blk + bf_tile:(j + 2) * blk]], axis=0)
                o = jnp.dot(p, win, preferred_element_type=F32)
                res = o if res is None else res + o
            pair_outs.append(res[:, :LANES] / res[:, LANES:])
        outs.append(jnp.concatenate(pair_outs, axis=-1))
    y = jnp.concatenate(outs, axis=0)
    o_ref[...] = _rms(y, gn_ref[...]).astype(o_ref.dtype)


def _out_ffn_kernel(x1_ref, ys_ref, ya_ref, wo_ref, g2_ref, wg_ref, wu_ref, wd_ref, gf_ref,
                    o_ref, *, fc):
    d_ssm = ys_ref.shape[-1]
    x2 = (x1_ref[...]
          + jnp.dot(ys_ref[...].astype(BF16), wo_ref[:d_ssm, :], preferred_element_type=F32)
          + jnp.dot(ya_ref[...], wo_ref[d_ssm:, :], preferred_element_type=F32))
    x3 = _swiglu_residual(x2, g2_ref, wg_ref, wu_ref, wd_ref, fc)
    o_ref[...] = _rms(x3, gf_ref[...])


def _const_spec(shape):
    nd = len(shape)
    return pl.BlockSpec(shape, lambda *_: (0,) * nd, pipeline_mode=pl.Buffered(1))


def _rope_lane_tables(seq):
    half = ROPE_DIM // 2
    inv_freq = ROPE_THETA ** (-jnp.arange(half, dtype=F32) * 2.0 / ROPE_DIM)
    ang = jnp.arange(seq, dtype=F32)[:, None] * inv_freq[None, :]
    cos, sin = jnp.cos(ang), jnp.sin(ang)
    ones = jnp.ones((seq, HEAD_DIM - ROPE_DIM), F32)
    zeros = jnp.zeros((seq, HEAD_DIM - ROPE_DIM), F32)
    zh = jnp.zeros((seq, half), F32)
    c = jnp.concatenate([cos, cos, ones], axis=-1)
    s1 = jnp.concatenate([zh, sin, zeros], axis=-1)
    s2 = jnp.concatenate([-sin, zh, zeros], axis=-1)
    reps = LANES // HEAD_DIM
    return tuple(jnp.tile(t, (1, reps)) for t in (c, s1, s2))


def _block_diag(m, n_halves):
    g, r, c = m.shape
    gh = g // n_halves
    m = m.reshape(n_halves, gh, r, c)
    eye = jnp.eye(gh, dtype=m.dtype)
    return jnp.einsum('hgrc,gk->hgrkc', m, eye).reshape(n_halves, gh * r, gh * c)


def _ssm_params(a_re, a_im, log_dt, b_re, b_im, c_re, c_im, n_chain):
    lam = lax.complex(a_re, a_im)
    dt = jnp.exp(log_dt)[:, None]
    lam_bar = jnp.exp(lam * dt)
    zoh = (lam_bar - 1.0) / lam
    b_bar = zoh[..., None] * lax.complex(b_re, b_im)
    n_state = lam_bar.size
    are = jnp.broadcast_to(lam_bar.real.reshape(1, n_state), (SUBLANES, n_state))
    aim = jnp.broadcast_to(lam_bar.imag.reshape(1, n_state), (SUBLANES, n_state))
    bre = _block_diag(jnp.swapaxes(b_bar.real, 1, 2), n_chain).astype(BF16)
    bim = _block_diag(jnp.swapaxes(b_bar.imag, 1, 2), n_chain).astype(BF16)
    cre = _block_diag(jnp.swapaxes(c_re, 1, 2), n_chain).astype(BF16)
    cim = _block_diag(jnp.swapaxes(-c_im, 1, 2), n_chain).astype(BF16)
    return are, aim, bre, bim, cre, cim


def _layer(x, cs, ffn1_norm, wg1, wu1, wd1, mix_norm, w_in, ssm_A_re, ssm_A_im, ssm_log_dt,
           ssm_B_re, ssm_B_im, ssm_C_re, ssm_C_im, ssm_D, ssm_w_glu, ssm_b_glu, attn_sinks,
           ssm_out_norm, attn_out_norm, w_out, ffn2_norm, wg2, wu2, wd2, final_norm):
    bsz, seq, d_model = x.shape
    d_ff = wg1.shape[1]
    d_ssm = ssm_D.shape[0]
    d_attn = attn_out_norm.shape[0]
    d_kv = N_KV_HEADS * HEAD_DIM
    d_kvp = 2 * N_KV_HEADS * LANES
    assert d_kv == LANES
    d_in = w_in.shape[1]
    tm = 1024
    fc = MXU_N
    tt = 64
    n_chain = 4
    nq = 4
    row = lambda v: v.reshape(1, -1).astype(F32)

    col_scale = jnp.concatenate([jnp.ones((d_ssm,), F32),
                                 jnp.full((d_attn,), LOG2E / math.sqrt(HEAD_DIM), F32),
                                 jnp.ones((2 * d_kv,), F32)])
    w_in_b = (w_in * col_scale[None, :]).astype(BF16)

    cparams = lambda sem: pltpu.CompilerParams(dimension_semantics=sem, vmem_limit_bytes=VMEM_LIMIT)

    tile = lambda w: pl.BlockSpec((None, tm, w), lambda b, i: (b, i, 0))
    rope_spec = pl.BlockSpec((tm, LANES), lambda b, i: (i, 0))
    tm_tile = pl.BlockSpec((tm, d_ssm), lambda b, i: (i, b))
    x1, u_tb, q, k, v = pl.pallas_call(
        functools.partial(_ffn_in_kernel, fc=fc, d_ssm=d_ssm, d_attn=d_attn, d_kv=d_kv),
        grid=(bsz, seq // tm),
        in_specs=[tile(d_model), _const_spec((1, d_model)), _const_spec((d_model, d_ff)),
                  _const_spec((d_model, d_ff)), _const_spec((d_ff, d_model)),
                  _const_spec((1, d_model)), _const_spec((d_model, d_in)),
                  rope_spec, rope_spec, rope_spec],
        out_specs=[tile(d_model),
                   tm_tile,
                   tile(d_attn), tile(d_kvp), tile(d_kvp)],
        out_shape=[jax.ShapeDtypeStruct((bsz, seq, d_model), F32),
                   jax.ShapeDtypeStruct((seq, bsz * d_ssm), F32),
                   jax.ShapeDtypeStruct((bsz, seq, d_attn), BF16),
                   jax.ShapeDtypeStruct((bsz, seq, d_kvp), BF16),
                   jax.ShapeDtypeStruct((bsz, seq, d_kvp), BF16)],
        compiler_params=cparams(("parallel", "parallel")),
        name="ffn_in",
    )(x, row(ffn1_norm), wg1.astype(BF16), wu1.astype(BF16), wd1.astype(BF16),
      row(mix_norm), w_in_b, *cs)

    assert bsz == SUBLANES
    are, aim, bre, bim, cre, cim = _ssm_params(ssm_A_re, ssm_A_im, ssm_log_dt,
                                               ssm_B_re, ssm_B_im, ssm_C_re, ssm_C_im, n_chain)
    n_state = are.shape[-1]
    u3 = u_tb.reshape(seq, bsz, d_ssm)
    n_steps = seq // (2 * tt)
    blk = (2 * tt, bsz, d_ssm)
    lag_spec = pl.BlockSpec(blk, lambda j: (jnp.maximum(j - 1, 0), 0, 0))
    y_ssm_tb = pl.pallas_call(
        functools.partial(_ssm_kernel, tt=tt, n_chain=n_chain),
        grid=(n_steps + 1,),
        in_specs=[pl.BlockSpec(blk, lambda j: (jnp.minimum(j, n_steps - 1), 0, 0)), lag_spec,
                  _const_spec(are.shape), _const_spec(aim.shape),
                  _const_spec(bre.shape), _const_spec(bim.shape),
                  _const_spec(cre.shape), _const_spec(cim.shape),
                  _const_spec((1, d_ssm)), _const_spec((d_ssm, d_ssm)),
                  _const_spec((1, d_ssm)), _const_spec((1, d_ssm))],
        out_specs=lag_spec,
        out_shape=jax.ShapeDtypeStruct((seq, bsz, d_ssm), F32),
        scratch_shapes=([pltpu.VMEM((bsz, n_state), F32)] * 2
                        + ([pltpu.VMEM((tt * bsz, n_state // n_chain), F32)] * (2 * n_chain)
                           + [pltpu.VMEM((tt * bsz, n_state // n_chain), BF16)] * (2 * n_chain)) * 2),
        compiler_params=cparams(("arbitrary",)),
        name="ssm",
    )(u3, u3, are, aim, bre, bim, cre, cim,
      row(ssm_D), ssm_w_glu.astype(BF16), row(ssm_b_glu), row(ssm_out_norm))

    tq = nq * WINDOW
    cur = lambda w: pl.BlockSpec((None, tq, w), lambda b, n: (b, n, 0))
    prev = lambda w: pl.BlockSpec((None, WINDOW, w), lambda b, n: (b, jnp.maximum(n * nq - 1, 0), 0))
    y_attn = pl.pallas_call(
        functools.partial(_attn_kernel, nq=nq),
        grid=(bsz, seq // tq),
        in_specs=[pl.BlockSpec(memory_space=pltpu.SMEM),
                  cur(d_attn), cur(d_kvp), prev(d_kvp), cur(d_kvp), prev(d_kvp),
                  pl.BlockSpec((1, d_attn), lambda b, n: (0, 0))],
        out_specs=cur(d_attn),
        out_shape=jax.ShapeDtypeStruct((bsz, seq, d_attn), BF16),
        compiler_params=cparams(("parallel", "parallel")),
        name="attn",
    )(attn_sinks.astype(F32), q, k, k, v, v, row(attn_out_norm))

    out = pl.pallas_call(
        functools.partial(_out_ffn_kernel, fc=fc),
        grid=(bsz, seq // tm),
        in_specs=[tile(d_model),
                  tm_tile,
                  tile(d_attn), _const_spec((d_ssm + d_attn, d_model)),
                  _const_spec((1, d_model)), _const_spec((d_model, d_ff)),
                  _const_spec((d_model, d_ff)), _const_spec((d_ff, d_model)),
                  _const_spec((1, d_model))],
        out_specs=tile(d_model),
        out_shape=jax.ShapeDtypeStruct((bsz, seq, d_model), F32),
        compiler_params=cparams(("parallel", "parallel")),
        name="out_ffn",
    )(x1, y_ssm_tb.reshape(seq, bsz * d_ssm), y_attn, w_out.astype(BF16),
      row(ffn2_norm), wg2.astype(BF16), wu2.astype(BF16), wd2.astype(BF16), row(final_norm))
    return out


def kernel(x, ffn1_norm, ffn1_w_gate, ffn1_w_up, ffn1_w_down, mix_norm, w_in, ssm_A_re, ssm_A_im, ssm_log_dt, ssm_B_re, ssm_B_im, ssm_C_re, ssm_C_im, ssm_D, ssm_w_glu, ssm_b_glu, attn_sinks, ssm_out_norm, attn_out_norm, w_out, ffn2_norm, ffn2_w_gate, ffn2_w_up, ffn2_w_down, final_norm):
    depth = ffn1_norm.shape[0]
    assert depth == 1, "the final norm is fused into the last layer's kernel"
    cs = _rope_lane_tables(x.shape[1])
    l = 0
    return _layer(x, cs, ffn1_norm[l], ffn1_w_gate[l], ffn1_w_up[l], ffn1_w_down[l], mix_norm[l],
                  w_in[l], ssm_A_re[l], ssm_A_im[l], ssm_log_dt[l], ssm_B_re[l], ssm_B_im[l],
                  ssm_C_re[l], ssm_C_im[l], ssm_D[l], ssm_w_glu[l], ssm_b_glu[l], attn_sinks[l],
                  ssm_out_norm[l], attn_out_norm[l], w_out[l], ffn2_norm[l], ffn2_w_gate[l],
                  ffn2_w_up[l], ffn2_w_down[l], final_norm)
```

```python
import functools
import math

import jax
import jax.numpy as jnp
from jax import lax
from jax.experimental import pallas as pl
from jax.experimental.pallas import tpu as pltpu

F32 = jnp.float32
BF16 = jnp.bfloat16

SSM_GROUP = 16
SSM_STATE = 64
HEAD_DIM = 64
N_KV_HEADS = 2
WINDOW = 128
ROPE_DIM = HEAD_DIM // 4
ROPE_THETA = 500000.0
RES_HALF = 0.5
EPS = 1e-6
NEG_INF = -1e30
LOG2E = math.log2(math.e)
LANES = 128
SUBLANES = 8
MXU_N = 256

VMEM_LIMIT = 56 * 1024 * 1024


def _rms(x, g):
    return x * lax.rsqrt(jnp.mean(x * x, axis=-1, keepdims=True) + EPS) * g


def _swiglu_residual(x, g_ref, wg_ref, wu_ref, wd_ref, fc):
    hn = _rms(x, g_ref[...]).astype(BF16)
    acc = x
    for c in range(wd_ref.shape[0] // fc):
        sl = slice(c * fc, (c + 1) * fc)
        g = jnp.dot(hn, wg_ref[:, sl], preferred_element_type=F32)
        u = jnp.dot(hn, wu_ref[:, sl], preferred_element_type=F32)
        a = (g * jax.nn.sigmoid(g) * u).astype(BF16)
        acc = acc + RES_HALF * jnp.dot(a, wd_ref[sl, :], preferred_element_type=F32)
    return acc


def _rope(t, c, s1, s2):
    half = ROPE_DIM // 2
    return t * c + pltpu.roll(t, half, 1) * s1 + pltpu.roll(t, LANES - half, 1) * s2


def _placed_kv(t):
    lane = lax.broadcasted_iota(jnp.int32, (1, LANES), 1)
    head0 = jnp.where(lane < HEAD_DIM, t, 0.0)
    head1 = jnp.where(lane < HEAD_DIM, 0.0, t)
    blocks = [head0, pltpu.roll(head0, HEAD_DIM, 1), pltpu.roll(head1, HEAD_DIM, 1), head1]
    return jnp.concatenate(blocks, axis=-1).astype(BF16)


def _ffn_in_kernel(x_ref, g1_ref, wg_ref, wu_ref, wd_ref, gm_ref, win_ref,
                   rc_ref, rs1_ref, rs2_ref,
                   x1_ref, u_ref, q_ref, k_ref, v_ref, *, fc, d_ssm, d_attn, d_kv):
    x1 = _swiglu_residual(x_ref[...], g1_ref, wg_ref, wu_ref, wd_ref, fc)
    x1_ref[...] = x1
    hn = _rms(x1, gm_ref[...]).astype(BF16)
    u_ref[...] = jnp.dot(hn, win_ref[:, :d_ssm], preferred_element_type=F32)
    c, s1, s2 = rc_ref[...], rs1_ref[...], rs2_ref[...]
    qkv = jnp.dot(hn, win_ref[:, d_ssm:], preferred_element_type=F32)
    for j in range(d_attn // LANES):
        sl = slice(j * LANES, (j + 1) * LANES)
        q_ref[:, sl] = _rope(qkv[:, sl], c, s1, s2).astype(BF16)
    k_ref[...] = _placed_kv(_rope(qkv[:, d_attn:d_attn + d_kv], c, s1, s2))
    v_ref[...] = _placed_kv(qkv[:, d_attn + d_kv:])


def _ssm_kernel(u_ref, are_ref, aim_ref, bre_ref, bim_ref, cre_ref, cim_ref,
                d_ref, wglu_ref, bglu_ref, gn_ref,
                y_ref, xre_ref, xim_ref, *s_refs, tt, n_chain):
    sre, sim = s_refs[:n_chain], s_refs[n_chain:]
    rows = tt * SUBLANES
    d_ssm = u_ref.shape[-1]
    cw = d_ssm // n_chain
    sw = are_ref.shape[-1] // n_chain

    @pl.when(pl.program_id(0) == 0)
    def _():
        xre_ref[...] = jnp.zeros_like(xre_ref)
        xim_ref[...] = jnp.zeros_like(xim_ref)

    u = u_ref[...].reshape(rows, d_ssm)
    ub = u.astype(BF16)
    for c in range(n_chain):
        uc = ub[:, c * cw:(c + 1) * cw]
        sre[c][...] = jnp.dot(uc, bre_ref[c], preferred_element_type=F32)
        sim[c][...] = jnp.dot(uc, bim_ref[c], preferred_element_type=F32)

    ys = []
    for c in range(n_chain):
        st = slice(c * sw, (c + 1) * sw)
        ar, ai = are_ref[:, st], aim_ref[:, st]
        xr, xi = xre_ref[:, st], xim_ref[:, st]
        for t in range(tt):
            r = slice(t * SUBLANES, (t + 1) * SUBLANES)
            xr, xi = (ar * xr - ai * xi + sre[c][r, :],
                      ar * xi + ai * xr + sim[c][r, :])
            sre[c][r, :] = xr
            sim[c][r, :] = xi
        xre_ref[:, st] = xr
        xim_ref[:, st] = xi
        ys.append(jnp.dot(sre[c][...].astype(BF16), cre_ref[c], preferred_element_type=F32)
                  + jnp.dot(sim[c][...].astype(BF16), cim_ref[c], preferred_element_type=F32))
    y = jnp.concatenate(ys, axis=-1) + d_ref[...] * u
    y = jax.nn.gelu(y)
    z = jnp.dot(y.astype(BF16), wglu_ref[...], preferred_element_type=F32) + bglu_ref[...]
    y = y * jax.nn.sigmoid(z)
    y_ref[...] = _rms(y, gn_ref[...]).reshape(tt, SUBLANES, d_ssm)


def _attn_kernel(sink_ref, q_ref, kc_ref, kp_ref, vc_ref, vp_ref, gn_ref, o_ref, *, nq):
    blk = WINDOW
    n = pl.program_id(1)
    lane = lax.broadcasted_iota(jnp.int32, (1, LANES), 1)
    k_all = jnp.concatenate([kp_ref[...], kc_ref[...]], axis=0)
    v_all = jnp.concatenate([vp_ref[...], vc_ref[...]], axis=0)

    bf_tile = 2 * SUBLANES
    row0 = lax.broadcasted_iota(jnp.int32, (bf_tile, 1), 0) == 0
    kmat, vext, vhead = [], [], []
    for kh in range(N_KV_HEADS):
        kmat.append([]), vext.append([]), vhead.append([])
        for half in range(2):
            sl = slice((kh * 2 + half) * LANES, (kh * 2 + half + 1) * LANES)
            ones = jnp.broadcast_to(jnp.where((lane // HEAD_DIM) == half, 1.0, 0.0),
                                    (v_all.shape[0], LANES)).astype(BF16)
            kmat[kh].append(k_all[:, sl])
            vext[kh].append(jnp.concatenate([v_all[:, sl], ones], axis=-1))
            vhead[kh].append([
                jnp.concatenate(
                    [jnp.where(row0, 0.0, v_all[j * blk:j * blk + bf_tile, sl].astype(F32)).astype(BF16),
                     ones[:bf_tile]], axis=-1)
                for j in range(nq)])

    qi = lax.broadcasted_iota(jnp.int32, (blk, 2 * blk), 0)
    sj = lax.broadcasted_iota(jnp.int32, (blk, 2 * blk), 1)
    band = (sj > qi) & (sj <= qi + blk)
    band_first = band & ((sj >= blk) | (n > 0))
    col0 = sj == 0
    n_pairs = q_ref.shape[-1] // LANES
    heads_per_kv = (2 * n_pairs) // N_KV_HEADS

    outs = []
    for j in range(nq):
        valid = band_first if j == 0 else band
        pair_outs = []
        for pr in range(n_pairs):
            qp = q_ref[j * blk:(j + 1) * blk, pr * LANES:(pr + 1) * LANES]
            res = None
            for half in range(2):
                h = 2 * pr + half
                kh = h // heads_per_kv
                fill = jnp.where(col0, sink_ref[h] * LOG2E, NEG_INF)
                s = lax.dot_general(qp, kmat[kh][half][j * blk:(j + 2) * blk],
                                    (((1,), (1,)), ((), ())), preferred_element_type=F32)
                s = jnp.where(valid, s, fill)
                m = jnp.max(s, axis=-1, keepdims=True)
                p = jnp.exp2(s - m).astype(BF16)
                win = jnp.concatenate([vhead[kh][half][j],
                                       vext[kh][half][j * blk + bf_tile:(j + 2) * blk]], axis=0)
                o = jnp.dot(p, win, preferred_element_type=F32)
                res = o if res is None else res + o
            pair_outs.append(res[:, :LANES] / res[:, LANES:])
        outs.append(jnp.concatenate(pair_outs, axis=-1))
    y = jnp.concatenate(outs, axis=0)
    o_ref[...] = _rms(y, gn_ref[...]).astype(o_ref.dtype)


def _out_ffn_kernel(x1_ref, ys_ref, ya_ref, wo_ref, g2_ref, wg_ref, wu_ref, wd_ref, gf_ref,
                    o_ref, *, fc):
    d_ssm = ys_ref.shape[-1]
    x2 = (x1_ref[...]
          + jnp.dot(ys_ref[...].astype(BF16), wo_ref[:d_ssm, :], preferred_element_type=F32)
          + jnp.dot(ya_ref[...], wo_ref[d_ssm:, :], preferred_element_type=F32))
    x3 = _swiglu_residual(x2, g2_ref, wg_ref, wu_ref, wd_ref, fc)
    o_ref[...] = _rms(x3, gf_ref[...])


def _const_spec(shape):
    nd = len(shape)
    return pl.BlockSpec(shape, lambda *_: (0,) * nd, pipeline_mode=pl.Buffered(1))


def _rope_lane_tables(seq):
    half = ROPE_DIM // 2
    inv_freq = ROPE_THETA ** (-jnp.arange(half, dtype=F32) * 2.0 / ROPE_DIM)
    ang = jnp.arange(seq, dtype=F32)[:, None] * inv_freq[None, :]
    cos, sin = jnp.cos(ang), jnp.sin(ang)
    ones = jnp.ones((seq, HEAD_DIM - ROPE_DIM), F32)
    zeros = jnp.zeros((seq, HEAD_DIM - ROPE_DIM), F32)
    zh = jnp.zeros((seq, half), F32)
    c = jnp.concatenate([cos, cos, ones], axis=-1)
    s1 = jnp.concatenate([zh, sin, zeros], axis=-1)
    s2 = jnp.concatenate([-sin, zh, zeros], axis=-1)
    reps = LANES // HEAD_DIM
    return tuple(jnp.tile(t, (1, reps)) for t in (c, s1, s2))


def _block_diag(m, n_halves):
    g, r, c = m.shape
    gh = g // n_halves
    m = m.reshape(n_halves, gh, r, c)
    eye = jnp.eye(gh, dtype=m.dtype)
    return jnp.einsum('hgrc,gk->hgrkc', m, eye).reshape(n_halves, gh * r, gh * c)


def _ssm_params(a_re, a_im, log_dt, b_re, b_im, c_re, c_im, n_chain):
    lam = lax.complex(a_re, a_im)
    dt = jnp.exp(log_dt)[:, None]
    lam_bar = jnp.exp(lam * dt)
    zoh = (lam_bar - 1.0) / lam
    b_bar = zoh[..., None] * lax.complex(b_re, b_im)
    n_state = lam_bar.size
    are = jnp.broadcast_to(lam_bar.real.reshape(1, n_state), (SUBLANES, n_state))
    aim = jnp.broadcast_to(lam_bar.imag.reshape(1, n_state), (SUBLANES, n_state))
    bre = _block_diag(jnp.swapaxes(b_bar.real, 1, 2), n_chain).astype(BF16)
    bim = _block_diag(jnp.swapaxes(b_bar.imag, 1, 2), n_chain).astype(BF16)
    cre = _block_diag(jnp.swapaxes(c_re, 1, 2), n_chain).astype(BF16)
    cim = _block_diag(jnp.swapaxes(-c_im, 1, 2), n_chain).astype(BF16)
    return are, aim, bre, bim, cre, cim


def _layer(x, cs, ffn1_norm, wg1, wu1, wd1, mix_norm, w_in, ssm_A_re, ssm_A_im, ssm_log_dt,
           ssm_B_re, ssm_B_im, ssm_C_re, ssm_C_im, ssm_D, ssm_w_glu, ssm_b_glu, attn_sinks,
           ssm_out_norm, attn_out_norm, w_out, ffn2_norm, wg2, wu2, wd2, final_norm):
    bsz, seq, d_model = x.shape
    d_ff = wg1.shape[1]
    d_ssm = ssm_D.shape[0]
    d_attn = attn_out_norm.shape[0]
    d_kv = N_KV_HEADS * HEAD_DIM
    d_kvp = 2 * N_KV_HEADS * LANES
    assert d_kv == LANES
    d_in = w_in.shape[1]
    tm = 1024
    fc = MXU_N
    tt = 64
    n_chain = 4
    nq = 8
    row = lambda v: v.reshape(1, -1).astype(F32)

    col_scale = jnp.concatenate([jnp.ones((d_ssm,), F32),
                                 jnp.full((d_attn,), LOG2E / math.sqrt(HEAD_DIM), F32),
                                 jnp.ones((2 * d_kv,), F32)])
    w_in_b = (w_in * col_scale[None, :]).astype(BF16)

    cparams = lambda sem: pltpu.CompilerParams(dimension_semantics=sem, vmem_limit_bytes=VMEM_LIMIT)

    tile = lambda w: pl.BlockSpec((None, tm, w), lambda b, i: (b, i, 0))
    rope_spec = pl.BlockSpec((tm, LANES), lambda b, i: (i, 0))
    tm_tile = pl.BlockSpec((tm, d_ssm), lambda b, i: (i, b))
    x1, u_tb, q, k, v = pl.pallas_call(
        functools.partial(_ffn_in_kernel, fc=fc, d_ssm=d_ssm, d_attn=d_attn, d_kv=d_kv),
        grid=(bsz, seq // tm),
        in_specs=[tile(d_model), _const_spec((1, d_model)), _const_spec((d_model, d_ff)),
                  _const_spec((d_model, d_ff)), _const_spec((d_ff, d_model)),
                  _const_spec((1, d_model)), _const_spec((d_model, d_in)),
                  rope_spec, rope_spec, rope_spec],
        out_specs=[tile(d_model),
                   tm_tile,
                   tile(d_attn), tile(d_kvp), tile(d_kvp)],
        out_shape=[jax.ShapeDtypeStruct((bsz, seq, d_model), F32),
                   jax.ShapeDtypeStruct((seq, bsz * d_ssm), F32),
                   jax.ShapeDtypeStruct((bsz, seq, d_attn), BF16),
                   jax.ShapeDtypeStruct((bsz, seq, d_kvp), BF16),
                   jax.ShapeDtypeStruct((bsz, seq, d_kvp), BF16)],
        compiler_params=cparams(("parallel", "parallel")),
        name="ffn_in",
    )(x, row(ffn1_norm), wg1.astype(BF16), wu1.astype(BF16), wd1.astype(BF16),
      row(mix_norm), w_in_b, *cs)

    assert bsz == SUBLANES
    are, aim, bre, bim, cre, cim = _ssm_params(ssm_A_re, ssm_A_im, ssm_log_dt,
                                               ssm_B_re, ssm_B_im, ssm_C_re, ssm_C_im, n_chain)
    n_state = are.shape[-1]
    tb_spec = pl.BlockSpec((tt, bsz, d_ssm), lambda i: (i, 0, 0))
    y_ssm_tb = pl.pallas_call(
        functools.partial(_ssm_kernel, tt=tt, n_chain=n_chain),
        grid=(seq // tt,),
        in_specs=[tb_spec, _const_spec(are.shape), _const_spec(aim.shape),
                  _const_spec(bre.shape), _const_spec(bim.shape),
                  _const_spec(cre.shape), _const_spec(cim.shape),
                  _const_spec((1, d_ssm)), _const_spec((d_ssm, d_ssm)),
                  _const_spec((1, d_ssm)), _const_spec((1, d_ssm))],
        out_specs=tb_spec,
        out_shape=jax.ShapeDtypeStruct((seq, bsz, d_ssm), F32),
        scratch_shapes=([pltpu.VMEM((bsz, n_state), F32)] * 2
                        + [pltpu.VMEM((tt * bsz, n_state // n_chain), F32)] * (2 * n_chain)),
        compiler_params=cparams(("arbitrary",)),
        name="ssm",
    )(u_tb.reshape(seq, bsz, d_ssm), are, aim, bre, bim, cre, cim,
      row(ssm_D), ssm_w_glu.astype(BF16), row(ssm_b_glu), row(ssm_out_norm))

    tq = nq * WINDOW
    cur = lambda w: pl.BlockSpec((None, tq, w), lambda b, n: (b, n, 0))
    prev = lambda w: pl.BlockSpec((None, WINDOW, w), lambda b, n: (b, jnp.maximum(n * nq - 1, 0), 0))
    y_attn = pl.pallas_call(
        functools.partial(_attn_kernel, nq=nq),
        grid=(bsz, seq // tq),
        in_specs=[pl.BlockSpec(memory_space=pltpu.SMEM),
                  cur(d_attn), cur(d_kvp), prev(d_kvp), cur(d_kvp), prev(d_kvp),
                  pl.BlockSpec((1, d_attn), lambda b, n: (0, 0))],
        out_specs=cur(d_attn),
        out_shape=jax.ShapeDtypeStruct((bsz, seq, d_attn), BF16),
        compiler_params=cparams(("parallel", "parallel")),
        name="attn",
    )(attn_sinks.astype(F32), q, k, k, v, v, row(attn_out_norm))

    out = pl.pallas_call(
        functools.partial(_out_ffn_kernel, fc=fc),
        grid=(bsz, seq // tm),
        in_specs=[tile(d_model),
                  tm_tile,
                  tile(d_attn), _const_spec((d_ssm + d_attn, d_model)),
                  _const_spec((1, d_model)), _const_spec((d_model, d_ff)),
                  _const_spec((d_model, d_ff)), _const_spec((d_ff, d_model)),
                  _const_spec((1, d_model))],
        out_specs=tile(d_model),
        out_shape=jax.ShapeDtypeStruct((bsz, seq, d_model), F32),
        compiler_params=cparams(("parallel", "parallel")),
        name="out_ffn",
    )(x1, y_ssm_tb.reshape(seq, bsz * d_ssm), y_attn, w_out.astype(BF16),
      row(ffn2_norm), wg2.astype(BF16), wu2.astype(BF16), wd2.astype(BF16), row(final_norm))
    return out


def kernel(x, ffn1_norm, ffn1_w_gate, ffn1_w_up, ffn1_w_down, mix_norm, w_in, ssm_A_re, ssm_A_im, ssm_log_dt, ssm_B_re, ssm_B_im, ssm_C_re, ssm_C_im, ssm_D, ssm_w_glu, ssm_b_glu, attn_sinks, ssm_out_norm, attn_out_norm, w_out, ffn2_norm, ffn2_w_gate, ffn2_w_up, ffn2_w_down, final_norm):
    depth = ffn1_norm.shape[0]
    assert depth == 1, "the final norm is fused into the last layer's kernel"
    cs = _rope_lane_tables(x.shape[1])
    l = 0
    return _layer(x, cs, ffn1_norm[l], ffn1_w_gate[l], ffn1_w_up[l], ffn1_w_down[l], mix_norm[l],
                  w_in[l], ssm_A_re[l], ssm_A_im[l], ssm_log_dt[l], ssm_B_re[l], ssm_B_im[l],
                  ssm_C_re[l], ssm_C_im[l], ssm_D[l], ssm_w_glu[l], ssm_b_glu[l], attn_sinks[l],
                  ssm_out_norm[l], attn_out_norm[l], w_out[l], ffn2_norm[l], ffn2_w_gate[l],
                  ffn2_w_up[l], ffn2_w_down[l], final_norm)
```

```python
import functools
import math

import jax
import jax.numpy as jnp
from jax import lax
from jax.experimental import pallas as pl
from jax.experimental.pallas import tpu as pltpu

F32 = jnp.float32
BF16 = jnp.bfloat16

SSM_GROUP = 16
SSM_STATE = 64
HEAD_DIM = 64
N_KV_HEADS = 2
WINDOW = 128
ROPE_DIM = HEAD_DIM // 4
ROPE_THETA = 500000.0
RES_HALF = 0.5
EPS = 1e-6
NEG_INF = -1e30
LOG2E = math.log2(math.e)
LANES = 128
SUBLANES = 8
MXU_N = 256

VMEM_LIMIT = 56 * 1024 * 1024


def _rms(x, g):
    return x * lax.rsqrt(jnp.mean(x * x, axis=-1, keepdims=True) + EPS) * g


def _swiglu_residual(x, g_ref, wg_ref, wu_ref, wd_ref, fc):
    hn = _rms(x, g_ref[...]).astype(BF16)
    acc = x
    d_ff = wd_ref.shape[0]
    chunks = [slice(lo, min(lo + fc, d_ff)) for lo in range(0, d_ff, fc)]

    def gate_up(sl):
        return (jnp.dot(hn, wg_ref[:, sl], preferred_element_type=F32),
                jnp.dot(hn, wu_ref[:, sl], preferred_element_type=F32))

    nxt = gate_up(chunks[0])
    for i, sl in enumerate(chunks):
        g, u = nxt
        if i + 1 < len(chunks):
            nxt = gate_up(chunks[i + 1])
        a = (g * jax.nn.sigmoid(g) * u).astype(BF16)
        acc = acc + RES_HALF * jnp.dot(a, wd_ref[sl, :], preferred_element_type=F32)
    return acc


def _rope(t, c, s1, s2):
    half = ROPE_DIM // 2
    return t * c + pltpu.roll(t, half, 1) * s1 + pltpu.roll(t, LANES - half, 1) * s2


def _placed_kv(t):
    lane = lax.broadcasted_iota(jnp.int32, (1, LANES), 1)
    head0 = jnp.where(lane < HEAD_DIM, t, 0.0)
    head1 = jnp.where(lane < HEAD_DIM, 0.0, t)
    blocks = [head0, pltpu.roll(head0, HEAD_DIM, 1), pltpu.roll(head1, HEAD_DIM, 1), head1]
    return jnp.concatenate(blocks, axis=-1).astype(BF16)


def _ffn_in_kernel(x_ref, g1_ref, wg_ref, wu_ref, wd_ref, gm_ref, win_ref,
                   rc_ref, rs1_ref, rs2_ref,
                   x1_ref, u_ref, q_ref, k_ref, v_ref, *, fc, d_ssm, d_attn, d_kv):
    x1 = _swiglu_residual(x_ref[...], g1_ref, wg_ref, wu_ref, wd_ref, fc)
    x1_ref[...] = x1
    hn = _rms(x1, gm_ref[...]).astype(BF16)
    u_ref[...] = jnp.dot(hn, win_ref[:, :d_ssm], preferred_element_type=F32)
    c, s1, s2 = rc_ref[...], rs1_ref[...], rs2_ref[...]
    qkv = jnp.dot(hn, win_ref[:, d_ssm:], preferred_element_type=F32)
    for j in range(d_attn // LANES):
        sl = slice(j * LANES, (j + 1) * LANES)
        q_ref[:, sl] = _rope(qkv[:, sl], c, s1, s2).astype(BF16)
    k_ref[...] = _placed_kv(_rope(qkv[:, d_attn:d_attn + d_kv], c, s1, s2))
    v_ref[...] = _placed_kv(qkv[:, d_attn + d_kv:])


def _ssm_kernel(u_ref, are_ref, aim_ref, bre_ref, bim_ref, cre_ref, cim_ref,
                d_ref, wglu_ref, bglu_ref, gn_ref,
                y_ref, xre_ref, xim_ref, tb_ref, *s_refs, tt, n_chain):
    sre, sim = s_refs[:n_chain], s_refs[n_chain:]
    rows = tt * SUBLANES
    d_ssm = d_ref.shape[-1]
    n_lane_blk = d_ssm // LANES
    cw = d_ssm // n_chain
    sw = are_ref.shape[-1] // n_chain

    @pl.when(pl.program_id(0) == 0)
    def _():
        xre_ref[...] = jnp.zeros_like(xre_ref)
        xim_ref[...] = jnp.zeros_like(xim_ref)

    for b in range(SUBLANES):
        for k in range(n_lane_blk):
            lo = b * d_ssm + k * LANES
            tb_ref[k, pl.ds(b, tt, stride=SUBLANES), :] = u_ref[:, lo:lo + LANES]
    u = jnp.concatenate([tb_ref[k] for k in range(n_lane_blk)], axis=-1)
    ub = u.astype(BF16)
    for c in range(n_chain):
        uc = ub[:, c * cw:(c + 1) * cw]
        sre[c][...] = jnp.dot(uc, bre_ref[c], preferred_element_type=F32)
        sim[c][...] = jnp.dot(uc, bim_ref[c], preferred_element_type=F32)

    ys = []
    for c in range(n_chain):
        st = slice(c * sw, (c + 1) * sw)
        ar, ai = are_ref[:, st], aim_ref[:, st]
        xr, xi = xre_ref[:, st], xim_ref[:, st]
        for t in range(tt):
            r = slice(t * SUBLANES, (t + 1) * SUBLANES)
            xr, xi = (ar * xr - ai * xi + sre[c][r, :],
                      ar * xi + ai * xr + sim[c][r, :])
            sre[c][r, :] = xr
            sim[c][r, :] = xi
        xre_ref[:, st] = xr
        xim_ref[:, st] = xi
        ys.append(jnp.dot(sre[c][...].astype(BF16), cre_ref[c], preferred_element_type=F32)
                  + jnp.dot(sim[c][...].astype(BF16), cim_ref[c], preferred_element_type=F32))
    y = jnp.concatenate(ys, axis=-1) + d_ref[...] * u
    y = jax.nn.gelu(y)
    z = jnp.dot(y.astype(BF16), wglu_ref[...], preferred_element_type=F32) + bglu_ref[...]
    y = y * jax.nn.sigmoid(z)
    y = _rms(y, gn_ref[...])
    for k in range(n_lane_blk):
        tb_ref[k] = y[:, k * LANES:(k + 1) * LANES]
    for b in range(SUBLANES):
        for k in range(n_lane_blk):
            lo = b * d_ssm + k * LANES
            y_ref[:, lo:lo + LANES] = tb_ref[k, pl.ds(b, tt, stride=SUBLANES), :]


def _attn_kernel(sink_ref, q_ref, kc_ref, kp_ref, vc_ref, vp_ref, gn_ref, o_ref, *, nq):
    blk = WINDOW
    n = pl.program_id(1)
    lane = lax.broadcasted_iota(jnp.int32, (1, LANES), 1)
    k_all = jnp.concatenate([kp_ref[...], kc_ref[...]], axis=0)
    v_all = jnp.concatenate([vp_ref[...], vc_ref[...]], axis=0)

    bf_tile = 2 * SUBLANES
    row0 = lax.broadcasted_iota(jnp.int32, (bf_tile, 1), 0) == 0
    kmat, vext, vhead = [], [], []
    for kh in range(N_KV_HEADS):
        kmat.append([]), vext.append([]), vhead.append([])
        for half in range(2):
            sl = slice((kh * 2 + half) * LANES, (kh * 2 + half + 1) * LANES)
            ones = jnp.broadcast_to(jnp.where((lane // HEAD_DIM) == half, 1.0, 0.0),
                                    (v_all.shape[0], LANES)).astype(BF16)
            kmat[kh].append(k_all[:, sl])
            vext[kh].append(jnp.concatenate([v_all[:, sl], ones], axis=-1))
            vhead[kh].append([
                jnp.concatenate(
                    [jnp.where(row0, 0.0, v_all[j * blk:j * blk + bf_tile, sl].astype(F32)).astype(BF16),
                     ones[:bf_tile]], axis=-1)
                for j in range(nq)])

    qi = lax.broadcasted_iota(jnp.int32, (blk, 2 * blk), 0)
    sj = lax.broadcasted_iota(jnp.int32, (blk, 2 * blk), 1)
    band = (sj > qi) & (sj <= qi + blk)
    band_first = band & ((sj >= blk) | (n > 0))
    col0 = sj == 0
    n_pairs = q_ref.shape[-1] // LANES
    heads_per_kv = (2 * n_pairs) // N_KV_HEADS

    outs = []
    for j in range(nq):
        valid = band_first if j == 0 else band
        pair_outs = []
        for pr in range(n_pairs):
            qp = q_ref[j * blk:(j + 1) * blk, pr * LANES:(pr + 1) * LANES]
            res = None
            for half in range(2):
                h = 2 * pr + half
                kh = h // heads_per_kv
                fill = jnp.where(col0, sink_ref[h] * LOG2E, NEG_INF)
                s = lax.dot_general(qp, kmat[kh][half][j * blk:(j + 2) * blk],
                                    (((1,), (1,)), ((), ())), preferred_element_type=F32)
                s = jnp.where(valid, s, fill)
                m = jnp.max(s, axis=-1, keepdims=True)
                p = jnp.exp2(s - m).astype(BF16)
                win = jnp.concatenate([vhead[kh][half][j],
                                       vext[kh][half][j * blk + bf_tile:(j + 2) * blk]], axis=0)
                o = jnp.dot(p, win, preferred_element_type=F32)
                res = o if res is None else res + o
            pair_outs.append(res[:, :LANES] / res[:, LANES:])
        outs.append(jnp.concatenate(pair_outs, axis=-1))
    y = jnp.concatenate(outs, axis=0)
    o_ref[...] = _rms(y, gn_ref[...]).astype(o_ref.dtype)


def _out_ffn_kernel(x1_ref, ys_ref, ya_ref, wo_ref, g2_ref, wg_ref, wu_ref, wd_ref, gf_ref,
                    o_ref, *, fc):
    d_ssm = ys_ref.shape[-1]
    x2 = (x1_ref[...]
          + jnp.dot(ys_ref[...].astype(BF16), wo_ref[:d_ssm, :], preferred_element_type=F32)
          + jnp.dot(ya_ref[...], wo_ref[d_ssm:, :], preferred_element_type=F32))
    x3 = _swiglu_residual(x2, g2_ref, wg_ref, wu_ref, wd_ref, fc)
    o_ref[...] = _rms(x3, gf_ref[...])


def _const_spec(shape):
    nd = len(shape)
    return pl.BlockSpec(shape, lambda *_: (0,) * nd, pipeline_mode=pl.Buffered(1))


def _rope_lane_tables(seq):
    half = ROPE_DIM // 2
    inv_freq = ROPE_THETA ** (-jnp.arange(half, dtype=F32) * 2.0 / ROPE_DIM)
    ang = jnp.arange(seq, dtype=F32)[:, None] * inv_freq[None, :]
    cos, sin = jnp.cos(ang), jnp.sin(ang)
    ones = jnp.ones((seq, HEAD_DIM - ROPE_DIM), F32)
    zeros = jnp.zeros((seq, HEAD_DIM - ROPE_DIM), F32)
    zh = jnp.zeros((seq, half), F32)
    c = jnp.concatenate([cos, cos, ones], axis=-1)
    s1 = jnp.concatenate([zh, sin, zeros], axis=-1)
    s2 = jnp.concatenate([-sin, zh, zeros], axis=-1)
    reps = LANES // HEAD_DIM
    return tuple(jnp.tile(t, (1, reps)) for t in (c, s1, s2))


def _block_diag(m, n_halves):
    g, r, c = m.shape
    gh = g // n_halves
    m = m.reshape(n_halves, gh, r, c)
    eye = jnp.eye(gh, dtype=m.dtype)
    return jnp.einsum('hgrc,gk->hgrkc', m, eye).reshape(n_halves, gh * r, gh * c)


def _discretize_kernel(are_ref, aim_ref, ldt_ref, bre_ref, bim_ref,
                       lre_ref, lim_ref, bbre_ref, bbim_ref):
    a_re, a_im = are_ref[...], aim_ref[...]
    dt = jnp.exp(ldt_ref[...])
    mag = jnp.exp(a_re * dt)
    l_re = mag * jnp.cos(a_im * dt)
    l_im = mag * jnp.sin(a_im * dt)
    lre_ref[...] = l_re
    lim_ref[...] = l_im
    n_re = l_re - 1.0
    den = a_re * a_re + a_im * a_im
    z_re = ((n_re * a_re + l_im * a_im) / den)[:, None, :]
    z_im = ((l_im * a_re - n_re * a_im) / den)[:, None, :]
    b_re, b_im = bre_ref[...], bim_ref[...]
    bbre_ref[...] = z_re * b_re - z_im * b_im
    bbim_ref[...] = z_re * b_im + z_im * b_re


def _ssm_params(a_re, a_im, log_dt, b_re, b_im, c_re, c_im, n_chain):
    g, p, c = b_re.shape
    lam_re, lam_im, bb_re, bb_im = pl.pallas_call(
        _discretize_kernel,
        out_shape=[jax.ShapeDtypeStruct((g, p), F32)] * 2 + [jax.ShapeDtypeStruct((g, c, p), F32)] * 2,
        name="ssm_discretize",
    )(a_re, a_im, log_dt.reshape(g, 1), jnp.swapaxes(b_re, 1, 2), jnp.swapaxes(b_im, 1, 2))
    n_state = g * p
    are = jnp.broadcast_to(lam_re.reshape(1, n_state), (SUBLANES, n_state))
    aim = jnp.broadcast_to(lam_im.reshape(1, n_state), (SUBLANES, n_state))
    bre = _block_diag(bb_re, n_chain).astype(BF16)
    bim = _block_diag(bb_im, n_chain).astype(BF16)
    cre = _block_diag(jnp.swapaxes(c_re, 1, 2), n_chain).astype(BF16)
    cim = _block_diag(jnp.swapaxes(-c_im, 1, 2), n_chain).astype(BF16)
    return are, aim, bre, bim, cre, cim


def _layer(x, cs, ffn1_norm, wg1, wu1, wd1, mix_norm, w_in, ssm_A_re, ssm_A_im, ssm_log_dt,
           ssm_B_re, ssm_B_im, ssm_C_re, ssm_C_im, ssm_D, ssm_w_glu, ssm_b_glu, attn_sinks,
           ssm_out_norm, attn_out_norm, w_out, ffn2_norm, wg2, wu2, wd2, final_norm):
    bsz, seq, d_model = x.shape
    d_ff = wg1.shape[1]
    d_ssm = ssm_D.shape[0]
    d_attn = attn_out_norm.shape[0]
    d_kv = N_KV_HEADS * HEAD_DIM
    d_kvp = 2 * N_KV_HEADS * LANES
    assert d_kv == LANES
    d_in = w_in.shape[1]
    tm = 1024
    fc = MXU_N
    tt = 64
    n_chain = 4
    nq = 8
    row = lambda v: v.reshape(1, -1).astype(F32)

    col_scale = jnp.concatenate([jnp.ones((d_ssm,), F32),
                                 jnp.full((d_attn,), LOG2E / math.sqrt(HEAD_DIM), F32),
                                 jnp.ones((2 * d_kv,), F32)])
    w_in_b = (w_in * col_scale[None, :]).astype(BF16)

    cparams = lambda sem: pltpu.CompilerParams(dimension_semantics=sem, vmem_limit_bytes=VMEM_LIMIT)

    tile = lambda w: pl.BlockSpec((None, tm, w), lambda b, i: (b, i, 0))
    rope_spec = pl.BlockSpec((tm, LANES), lambda b, i: (i, 0))
    tm_tile = pl.BlockSpec((tm, d_ssm), lambda b, i: (i, b))
    x1, u_tb, q, k, v = pl.pallas_call(
        functools.partial(_ffn_in_kernel, fc=fc, d_ssm=d_ssm, d_attn=d_attn, d_kv=d_kv),
        grid=(bsz, seq // tm),
        in_specs=[tile(d_model), _const_spec((1, d_model)), _const_spec((d_model, d_ff)),
                  _const_spec((d_model, d_ff)), _const_spec((d_ff, d_model)),
                  _const_spec((1, d_model)), _const_spec((d_model, d_in)),
                  rope_spec, rope_spec, rope_spec],
        out_specs=[tile(d_model),
                   tm_tile,
                   tile(d_attn), tile(d_kvp), tile(d_kvp)],
        out_shape=[jax.ShapeDtypeStruct((bsz, seq, d_model), F32),
                   jax.ShapeDtypeStruct((seq, bsz * d_ssm), F32),
                   jax.ShapeDtypeStruct((bsz, seq, d_attn), BF16),
                   jax.ShapeDtypeStruct((bsz, seq, d_kvp), BF16),
                   jax.ShapeDtypeStruct((bsz, seq, d_kvp), BF16)],
        compiler_params=cparams(("parallel", "parallel")),
        name="ffn_in",
    )(x, row(ffn1_norm), wg1.astype(BF16), wu1.astype(BF16), wd1.astype(BF16),
      row(mix_norm), w_in_b, *cs)

    assert bsz == SUBLANES
    are, aim, bre, bim, cre, cim = _ssm_params(ssm_A_re, ssm_A_im, ssm_log_dt,
                                               ssm_B_re, ssm_B_im, ssm_C_re, ssm_C_im, n_chain)
    n_state = are.shape[-1]
    tb_spec = pl.BlockSpec((tt, bsz * d_ssm), lambda i: (i, 0))
    y_ssm_tb = pl.pallas_call(
        functools.partial(_ssm_kernel, tt=tt, n_chain=n_chain),
        grid=(seq // tt,),
        in_specs=[tb_spec, _const_spec(are.shape), _const_spec(aim.shape),
                  _const_spec(bre.shape), _const_spec(bim.shape),
                  _const_spec(cre.shape), _const_spec(cim.shape),
                  _const_spec((1, d_ssm)), _const_spec((d_ssm, d_ssm)),
                  _const_spec((1, d_ssm)), _const_spec((1, d_ssm))],
        out_specs=tb_spec,
        out_shape=jax.ShapeDtypeStruct((seq, bsz * d_ssm), F32),
        scratch_shapes=([pltpu.VMEM((bsz, n_state), F32)] * 2
                        + [pltpu.VMEM((d_ssm // LANES, tt * bsz, LANES), F32)]
                        + [pltpu.VMEM((tt * bsz, n_state // n_chain), F32)] * (2 * n_chain)),
        compiler_params=cparams(("arbitrary",)),
        name="ssm",
    )(u_tb, are, aim, bre, bim, cre, cim,
      row(ssm_D), ssm_w_glu.astype(BF16), row(ssm_b_glu), row(ssm_out_norm))

    tq = nq * WINDOW
    cur = lambda w: pl.BlockSpec((None, tq, w), lambda b, n: (b, n, 0))
    prev = lambda w: pl.BlockSpec((None, WINDOW, w), lambda b, n: (b, jnp.maximum(n * nq - 1, 0), 0))
    y_attn = pl.pallas_call(
        functools.partial(_attn_kernel, nq=nq),
        grid=(bsz, seq // tq),
        in_specs=[pl.BlockSpec(memory_space=pltpu.SMEM),
                  cur(d_attn), cur(d_kvp), prev(d_kvp), cur(d_kvp), prev(d_kvp),
                  pl.BlockSpec((1, d_attn), lambda b, n: (0, 0))],
        out_specs=cur(d_attn),
        out_shape=jax.ShapeDtypeStruct((bsz, seq, d_attn), BF16),
        compiler_params=cparams(("parallel", "parallel")),
        name="attn",
    )(attn_sinks.astype(F32), q, k, k, v, v, row(attn_out_norm))

    out = pl.pallas_call(
        functools.partial(_out_ffn_kernel, fc=fc),
        grid=(bsz, seq // tm),
        in_specs=[tile(d_model),
                  tm_tile,
                  tile(d_attn), _const_spec((d_ssm + d_attn, d_model)),
                  _const_spec((1, d_model)), _const_spec((d_model, d_ff)),
                  _const_spec((d_model, d_ff)), _const_spec((d_ff, d_model)),
                  _const_spec((1, d_model))],
        out_specs=tile(d_model),
        out_shape=jax.ShapeDtypeStruct((bsz, seq, d_model), F32),
        compiler_params=cparams(("parallel", "parallel")),
        name="out_ffn",
    )(x1, y_ssm_tb, y_attn, w_out.astype(BF16),
      row(ffn2_norm), wg2.astype(BF16), wu2.astype(BF16), wd2.astype(BF16), row(final_norm))
    return out


def kernel(x, ffn1_norm, ffn1_w_gate, ffn1_w_up, ffn1_w_down, mix_norm, w_in, ssm_A_re, ssm_A_im, ssm_log_dt, ssm_B_re, ssm_B_im, ssm_C_re, ssm_C_im, ssm_D, ssm_w_glu, ssm_b_glu, attn_sinks, ssm_out_norm, attn_out_norm, w_out, ffn2_norm, ffn2_w_gate, ffn2_w_up, ffn2_w_down, final_norm):
    depth = ffn1_norm.shape[0]
    assert depth == 1, "the final norm is fused into the last layer's kernel"
    cs = _rope_lane_tables(x.shape[1])
    l = 0
    return _layer(x, cs, ffn1_norm[l], ffn1_w_gate[l], ffn1_w_up[l], ffn1_w_down[l], mix_norm[l],
                  w_in[l], ssm_A_re[l], ssm_A_im[l], ssm_log_dt[l], ssm_B_re[l], ssm_B_im[l],
                  ssm_C_re[l], ssm_C_im[l], ssm_D[l], ssm_w_glu[l], ssm_b_glu[l], attn_sinks[l],
                  ssm_out_norm[l], attn_out_norm[l], w_out[l], ffn2_norm[l], ffn2_w_gate[l],
                  ffn2_w_up[l], ffn2_w_down[l], final_norm)
```

```python
import functools
import math

import jax
import jax.numpy as jnp
from jax import lax
from jax.experimental import pallas as pl
from jax.experimental.pallas import tpu as pltpu

F32 = jnp.float32
BF16 = jnp.bfloat16

SSM_GROUP = 16
SSM_STATE = 64
HEAD_DIM = 64
N_KV_HEADS = 2
WINDOW = 128
ROPE_DIM = HEAD_DIM // 4
ROPE_THETA = 500000.0
RES_HALF = 0.5
EPS = 1e-6
NEG_INF = -1e30
LOG2E = math.log2(math.e)
LANES = 128
SUBLANES = 8
MXU_N = 256

VMEM_LIMIT = 56 * 1024 * 1024


def _rms(x, g):
    return x * lax.rsqrt(jnp.mean(x * x, axis=-1, keepdims=True) + EPS) * g


def _swiglu_residual(x, g_ref, wg_ref, wu_ref, wd_ref, fc):
    hn = _rms(x, g_ref[...]).astype(BF16)
    acc = x
    d_ff = wd_ref.shape[0]
    chunks = [slice(lo, min(lo + fc, d_ff)) for lo in range(0, d_ff, fc)]

    def gate_up(sl):
        return (jnp.dot(hn, wg_ref[:, sl], preferred_element_type=F32),
                jnp.dot(hn, wu_ref[:, sl], preferred_element_type=F32))

    nxt = gate_up(chunks[0])
    for i, sl in enumerate(chunks):
        g, u = nxt
        if i + 1 < len(chunks):
            nxt = gate_up(chunks[i + 1])
        a = (g * jax.nn.sigmoid(g) * u).astype(BF16)
        acc = acc + RES_HALF * jnp.dot(a, wd_ref[sl, :], preferred_element_type=F32)
    return acc


def _rope(t, c, s1, s2):
    half = ROPE_DIM // 2
    return t * c + pltpu.roll(t, half, 1) * s1 + pltpu.roll(t, LANES - half, 1) * s2


def _placed_kv(t):
    lane = lax.broadcasted_iota(jnp.int32, (1, LANES), 1)
    head0 = jnp.where(lane < HEAD_DIM, t, 0.0)
    head1 = jnp.where(lane < HEAD_DIM, 0.0, t)
    blocks = [head0, pltpu.roll(head0, HEAD_DIM, 1), pltpu.roll(head1, HEAD_DIM, 1), head1]
    return jnp.concatenate(blocks, axis=-1).astype(BF16)


def _ffn_in_kernel(x_ref, g1_ref, wg_ref, wu_ref, wd_ref, gm_ref, win_ref,
                   rc_ref, rs1_ref, rs2_ref,
                   x1_ref, u_ref, q_ref, k_ref, v_ref, *, fc, d_ssm, d_attn, d_kv):
    x1 = _swiglu_residual(x_ref[...], g1_ref, wg_ref, wu_ref, wd_ref, fc)
    x1_ref[...] = x1
    hn = _rms(x1, gm_ref[...]).astype(BF16)
    c, s1, s2 = rc_ref[...], rs1_ref[...], rs2_ref[...]
    half = x_ref.shape[0] // 2
    for rs in (slice(0, half), slice(half, 2 * half)):
        kv = jnp.dot(hn[rs], win_ref[:, d_ssm + d_attn:], preferred_element_type=F32)
        k_ref[rs, :] = _placed_kv(_rope(kv[:, :d_kv], c[rs], s1[rs], s2[rs]))
        v_ref[rs, :] = _placed_kv(kv[:, d_kv:])
    q = jnp.dot(hn, win_ref[:, d_ssm:d_ssm + d_attn], preferred_element_type=F32)
    for j in range(d_attn // LANES):
        sl = slice(j * LANES, (j + 1) * LANES)
        q_ref[:, sl] = _rope(q[:, sl], c, s1, s2).astype(BF16)
    u_ref[...] = jnp.dot(hn, win_ref[:, :d_ssm], preferred_element_type=F32)


def _ssm_kernel(u_ref, are_ref, aim_ref, bre_ref, bim_ref, cre_ref, cim_ref,
                d_ref, wglu_ref, bglu_ref, gn_ref,
                y_ref, xre_ref, xim_ref, tb_ref, *s_refs, tt, n_chain):
    sre, sim = s_refs[:n_chain], s_refs[n_chain:]
    rows = tt * SUBLANES
    d_ssm = d_ref.shape[-1]
    n_lane_blk = d_ssm // LANES
    cw = d_ssm // n_chain
    sw = are_ref.shape[-1] // n_chain

    @pl.when(pl.program_id(0) == 0)
    def _():
        xre_ref[...] = jnp.zeros_like(xre_ref)
        xim_ref[...] = jnp.zeros_like(xim_ref)

    for b in range(SUBLANES):
        for k in range(n_lane_blk):
            lo = b * d_ssm + k * LANES
            tb_ref[k, pl.ds(b, tt, stride=SUBLANES), :] = u_ref[:, lo:lo + LANES]
    u = jnp.concatenate([tb_ref[k] for k in range(n_lane_blk)], axis=-1)
    ub = u.astype(BF16)
    for c in range(n_chain):
        uc = ub[:, c * cw:(c + 1) * cw]
        sre[c][...] = jnp.dot(uc, bre_ref[c], preferred_element_type=F32)
        sim[c][...] = jnp.dot(uc, bim_ref[c], preferred_element_type=F32)

    ys = []
    for c in range(n_chain):
        st = slice(c * sw, (c + 1) * sw)
        ar, ai = are_ref[:, st], aim_ref[:, st]
        xr, xi = xre_ref[:, st], xim_ref[:, st]
        for t in range(tt):
            r = slice(t * SUBLANES, (t + 1) * SUBLANES)
            xr, xi = (ar * xr - ai * xi + sre[c][r, :],
                      ar * xi + ai * xr + sim[c][r, :])
            sre[c][r, :] = xr
            sim[c][r, :] = xi
        xre_ref[:, st] = xr
        xim_ref[:, st] = xi
        ys.append(jnp.dot(sre[c][...].astype(BF16), cre_ref[c], preferred_element_type=F32)
                  + jnp.dot(sim[c][...].astype(BF16), cim_ref[c], preferred_element_type=F32))
    y = jnp.concatenate(ys, axis=-1) + d_ref[...] * u
    y = jax.nn.gelu(y)
    z = jnp.dot(y.astype(BF16), wglu_ref[...], preferred_element_type=F32) + bglu_ref[...]
    y = y * jax.nn.sigmoid(z)
    y = _rms(y, gn_ref[...])
    for k in range(n_lane_blk):
        tb_ref[k] = y[:, k * LANES:(k + 1) * LANES]
    for b in range(SUBLANES):
        for k in range(n_lane_blk):
            lo = b * d_ssm + k * LANES
            y_ref[:, lo:lo + LANES] = tb_ref[k, pl.ds(b, tt, stride=SUBLANES), :]


def _attn_kernel(sink_ref, q_ref, kc_ref, kp_ref, vc_ref, vp_ref, gn_ref, o_ref, *, nq):
    blk = WINDOW
    n = pl.program_id(1)
    lane = lax.broadcasted_iota(jnp.int32, (1, LANES), 1)
    k_all = jnp.concatenate([kp_ref[...], kc_ref[...]], axis=0)
    v_all = jnp.concatenate([vp_ref[...], vc_ref[...]], axis=0)

    bf_tile = 2 * SUBLANES
    row0 = lax.broadcasted_iota(jnp.int32, (bf_tile, 1), 0) == 0
    kmat, vext, vhead = [], [], []
    for kh in range(N_KV_HEADS):
        kmat.append([]), vext.append([]), vhead.append([])
        for half in range(2):
            sl = slice((kh * 2 + half) * LANES, (kh * 2 + half + 1) * LANES)
            ones = jnp.broadcast_to(jnp.where((lane // HEAD_DIM) == half, 1.0, 0.0),
                                    (v_all.shape[0], LANES)).astype(BF16)
            kmat[kh].append(k_all[:, sl])
            vext[kh].append(jnp.concatenate([v_all[:, sl], ones], axis=-1))
            vhead[kh].append([
                jnp.concatenate(
                    [jnp.where(row0, 0.0, v_all[j * blk:j * blk + bf_tile, sl].astype(F32)).astype(BF16),
                     ones[:bf_tile]], axis=-1)
                for j in range(nq)])

    qi = lax.broadcasted_iota(jnp.int32, (blk, 2 * blk), 0)
    sj = lax.broadcasted_iota(jnp.int32, (blk, 2 * blk), 1)
    band = (sj > qi) & (sj <= qi + blk)
    band_first = band & ((sj >= blk) | (n > 0))
    col0 = sj == 0
    n_pairs = q_ref.shape[-1] // LANES
    heads_per_kv = (2 * n_pairs) // N_KV_HEADS

    outs = []
    for j in range(nq):
        valid = band_first if j == 0 else band
        pair_outs = []
        for pr in range(n_pairs):
            qp = q_ref[j * blk:(j + 1) * blk, pr * LANES:(pr + 1) * LANES]
            res = None
            for half in range(2):
                h = 2 * pr + half
                kh = h // heads_per_kv
                fill = jnp.where(col0, sink_ref[h] * LOG2E, NEG_INF)
                s = lax.dot_general(qp, kmat[kh][half][j * blk:(j + 2) * blk],
                                    (((1,), (1,)), ((), ())), preferred_element_type=F32)
                s = jnp.where(valid, s, fill)
                m = jnp.max(s, axis=-1, keepdims=True)
                p = jnp.exp2(s - m).astype(BF16)
                win = jnp.concatenate([vhead[kh][half][j],
                                       vext[kh][half][j * blk + bf_tile:(j + 2) * blk]], axis=0)
                o = jnp.dot(p, win, preferred_element_type=F32)
                res = o if res is None else res + o
            pair_outs.append(res[:, :LANES] / res[:, LANES:])
        outs.append(jnp.concatenate(pair_outs, axis=-1))
    y = jnp.concatenate(outs, axis=0)
    o_ref[...] = _rms(y, gn_ref[...]).astype(o_ref.dtype)


def _out_ffn_kernel(x1_ref, ys_ref, ya_ref, wo_ref, g2_ref, wg_ref, wu_ref, wd_ref, gf_ref,
                    o_ref, *, fc):
    d_ssm = ys_ref.shape[-1]
    x2 = (x1_ref[...]
          + jnp.dot(ys_ref[...].astype(BF16), wo_ref[:d_ssm, :], preferred_element_type=F32)
          + jnp.dot(ya_ref[...], wo_ref[d_ssm:, :], preferred_element_type=F32))
    x3 = _swiglu_residual(x2, g2_ref, wg_ref, wu_ref, wd_ref, fc)
    o_ref[...] = _rms(x3, gf_ref[...])


def _const_spec(shape):
    nd = len(shape)
    return pl.BlockSpec(shape, lambda *_: (0,) * nd, pipeline_mode=pl.Buffered(1))


def _rope_lane_tables(seq):
    half = ROPE_DIM // 2
    inv_freq = ROPE_THETA ** (-jnp.arange(half, dtype=F32) * 2.0 / ROPE_DIM)
    ang = jnp.arange(seq, dtype=F32)[:, None] * inv_freq[None, :]
    cos, sin = jnp.cos(ang), jnp.sin(ang)
    ones = jnp.ones((seq, HEAD_DIM - ROPE_DIM), F32)
    zeros = jnp.zeros((seq, HEAD_DIM - ROPE_DIM), F32)
    zh = jnp.zeros((seq, half), F32)
    c = jnp.concatenate([cos, cos, ones], axis=-1)
    s1 = jnp.concatenate([zh, sin, zeros], axis=-1)
    s2 = jnp.concatenate([-sin, zh, zeros], axis=-1)
    reps = LANES // HEAD_DIM
    return tuple(jnp.tile(t, (1, reps)) for t in (c, s1, s2))


def _block_diag(m, n_halves):
    g, r, c = m.shape
    gh = g // n_halves
    m = m.reshape(n_halves, gh, r, c)
    eye = jnp.eye(gh, dtype=m.dtype)
    return jnp.einsum('hgrc,gk->hgrkc', m, eye).reshape(n_halves, gh * r, gh * c)


def _discretize_kernel(are_ref, aim_ref, ldt_ref, bre_ref, bim_ref,
                       lre_ref, lim_ref, bbre_ref, bbim_ref):
    a_re, a_im = are_ref[...], aim_ref[...]
    dt = jnp.exp(ldt_ref[...])
    mag = jnp.exp(a_re * dt)
    l_re = mag * jnp.cos(a_im * dt)
    l_im = mag * jnp.sin(a_im * dt)
    lre_ref[...] = l_re
    lim_ref[...] = l_im
    n_re = l_re - 1.0
    den = a_re * a_re + a_im * a_im
    z_re = ((n_re * a_re + l_im * a_im) / den)[:, None, :]
    z_im = ((l_im * a_re - n_re * a_im) / den)[:, None, :]
    b_re, b_im = bre_ref[...], bim_ref[...]
    bbre_ref[...] = z_re * b_re - z_im * b_im
    bbim_ref[...] = z_re * b_im + z_im * b_re


def _ssm_params(a_re, a_im, log_dt, b_re, b_im, c_re, c_im, n_chain):
    g, p, c = b_re.shape
    lam_re, lam_im, bb_re, bb_im = pl.pallas_call(
        _discretize_kernel,
        out_shape=[jax.ShapeDtypeStruct((g, p), F32)] * 2 + [jax.ShapeDtypeStruct((g, c, p), F32)] * 2,
        name="ssm_discretize",
    )(a_re, a_im, log_dt.reshape(g, 1), jnp.swapaxes(b_re, 1, 2), jnp.swapaxes(b_im, 1, 2))
    n_state = g * p
    are = jnp.broadcast_to(lam_re.reshape(1, n_state), (SUBLANES, n_state))
    aim = jnp.broadcast_to(lam_im.reshape(1, n_state), (SUBLANES, n_state))
    bre = _block_diag(bb_re, n_chain).astype(BF16)
    bim = _block_diag(bb_im, n_chain).astype(BF16)
    cre = _block_diag(jnp.swapaxes(c_re, 1, 2), n_chain).astype(BF16)
    cim = _block_diag(jnp.swapaxes(-c_im, 1, 2), n_chain).astype(BF16)
    return are, aim, bre, bim, cre, cim


def _layer(x, cs, ffn1_norm, wg1, wu1, wd1, mix_norm, w_in, ssm_A_re, ssm_A_im, ssm_log_dt,
           ssm_B_re, ssm_B_im, ssm_C_re, ssm_C_im, ssm_D, ssm_w_glu, ssm_b_glu, attn_sinks,
           ssm_out_norm, attn_out_norm, w_out, ffn2_norm, wg2, wu2, wd2, final_norm):
    bsz, seq, d_model = x.shape
    d_ff = wg1.shape[1]
    d_ssm = ssm_D.shape[0]
    d_attn = attn_out_norm.shape[0]
    d_kv = N_KV_HEADS * HEAD_DIM
    d_kvp = 2 * N_KV_HEADS * LANES
    assert d_kv == LANES
    d_in = w_in.shape[1]
    tm = 1024
    fc = MXU_N
    tt = 64
    n_chain = 4
    nq = 8
    row = lambda v: v.reshape(1, -1).astype(F32)

    col_scale = jnp.concatenate([jnp.ones((d_ssm,), F32),
                                 jnp.full((d_attn,), LOG2E / math.sqrt(HEAD_DIM), F32),
                                 jnp.ones((2 * d_kv,), F32)])
    w_in_b = (w_in * col_scale[None, :]).astype(BF16)

    cparams = lambda sem: pltpu.CompilerParams(dimension_semantics=sem, vmem_limit_bytes=VMEM_LIMIT)

    tile = lambda w: pl.BlockSpec((None, tm, w), lambda b, i: (b, i, 0))
    rope_spec = pl.BlockSpec((tm, LANES), lambda b, i: (i, 0))
    tm_tile = pl.BlockSpec((tm, d_ssm), lambda b, i: (i, b))
    x1, u_tb, q, k, v = pl.pallas_call(
        functools.partial(_ffn_in_kernel, fc=fc, d_ssm=d_ssm, d_attn=d_attn, d_kv=d_kv),
        grid=(bsz, seq // tm),
        in_specs=[tile(d_model), _const_spec((1, d_model)), _const_spec((d_model, d_ff)),
                  _const_spec((d_model, d_ff)), _const_spec((d_ff, d_model)),
                  _const_spec((1, d_model)), _const_spec((d_model, d_in)),
                  rope_spec, rope_spec, rope_spec],
        out_specs=[tile(d_model),
                   tm_tile,
                   tile(d_attn), tile(d_kvp), tile(d_kvp)],
        out_shape=[jax.ShapeDtypeStruct((bsz, seq, d_model), F32),
                   jax.ShapeDtypeStruct((seq, bsz * d_ssm), F32),
                   jax.ShapeDtypeStruct((bsz, seq, d_attn), BF16),
                   jax.ShapeDtypeStruct((bsz, seq, d_kvp), BF16),
                   jax.ShapeDtypeStruct((bsz, seq, d_kvp), BF16)],
        compiler_params=cparams(("parallel", "parallel")),
        name="ffn_in",
    )(x, row(ffn1_norm), wg1.astype(BF16), wu1.astype(BF16), wd1.astype(BF16),
      row(mix_norm), w_in_b, *cs)

    assert bsz == SUBLANES
    are, aim, bre, bim, cre, cim = _ssm_params(ssm_A_re, ssm_A_im, ssm_log_dt,
                                               ssm_B_re, ssm_B_im, ssm_C_re, ssm_C_im, n_chain)
    n_state = are.shape[-1]
    tb_spec = pl.BlockSpec((tt, bsz * d_ssm), lambda i: (i, 0))
    y_ssm_tb = pl.pallas_call(
        functools.partial(_ssm_kernel, tt=tt, n_chain=n_chain),
        grid=(seq // tt,),
        in_specs=[tb_spec, _const_spec(are.shape), _const_spec(aim.shape),
                  _const_spec(bre.shape), _const_spec(bim.shape),
                  _const_spec(cre.shape), _const_spec(cim.shape),
                  _const_spec((1, d_ssm)), _const_spec((d_ssm, d_ssm)),
                  _const_spec((1, d_ssm)), _const_spec((1, d_ssm))],
        out_specs=tb_spec,
        out_shape=jax.ShapeDtypeStruct((seq, bsz * d_ssm), F32),
        scratch_shapes=([pltpu.VMEM((bsz, n_state), F32)] * 2
                        + [pltpu.VMEM((d_ssm // LANES, tt * bsz, LANES), F32)]
                        + [pltpu.VMEM((tt * bsz, n_state // n_chain), F32)] * (2 * n_chain)),
        compiler_params=cparams(("arbitrary",)),
        name="ssm",
    )(u_tb, are, aim, bre, bim, cre, cim,
      row(ssm_D), ssm_w_glu.astype(BF16), row(ssm_b_glu), row(ssm_out_norm))

    tq = nq * WINDOW
    cur = lambda w: pl.BlockSpec((None, tq, w), lambda b, n: (b, n, 0))
    prev = lambda w: pl.BlockSpec((None, WINDOW, w), lambda b, n: (b, jnp.maximum(n * nq - 1, 0), 0))
    y_attn = pl.pallas_call(
        functools.partial(_attn_kernel, nq=nq),
        grid=(bsz, seq // tq),
        in_specs=[pl.BlockSpec(memory_space=pltpu.SMEM),
                  cur(d_attn), cur(d_kvp), prev(d_kvp), cur(d_kvp), prev(d_kvp),
                  pl.BlockSpec((1, d_attn), lambda b, n: (0, 0))],
        out_specs=cur(d_attn),
        out_shape=jax.ShapeDtypeStruct((bsz, seq, d_attn), BF16),
        compiler_params=cparams(("parallel", "parallel")),
        name="attn",
    )(attn_sinks.astype(F32), q, k, k, v, v, row(attn_out_norm))

    out = pl.pallas_call(
        functools.partial(_out_ffn_kernel, fc=fc),
        grid=(bsz, seq // tm),
        in_specs=[tile(d_model),
                  tm_tile,
                  tile(d_attn), _const_spec((d_ssm + d_attn, d_model)),
                  _const_spec((1, d_model)), _const_spec((d_model, d_ff)),
                  _const_spec((d_model, d_ff)), _const_spec((d_ff, d_model)),
                  _const_spec((1, d_model))],
        out_specs=tile(d_model),
        out_shape=jax.ShapeDtypeStruct((bsz, seq, d_model), F32),
        compiler_params=cparams(("parallel", "parallel")),
        name="out_ffn",
    )(x1, y_ssm_tb, y_attn, w_out.astype(BF16),
      row(ffn2_norm), wg2.astype(BF16), wu2.astype(BF16), wd2.astype(BF16), row(final_norm))
    return out


def kernel(x, ffn1_norm, ffn1_w_gate, ffn1_w_up, ffn1_w_down, mix_norm, w_in, ssm_A_re, ssm_A_im, ssm_log_dt, ssm_B_re, ssm_B_im, ssm_C_re, ssm_C_im, ssm_D, ssm_w_glu, ssm_b_glu, attn_sinks, ssm_out_norm, attn_out_norm, w_out, ffn2_norm, ffn2_w_gate, ffn2_w_up, ffn2_w_down, final_norm):
    depth = ffn1_norm.shape[0]
    assert depth == 1, "the final norm is fused into the last layer's kernel"
    cs = _rope_lane_tables(x.shape[1])
    l = 0
    return _layer(x, cs, ffn1_norm[l], ffn1_w_gate[l], ffn1_w_up[l], ffn1_w_down[l], mix_norm[l],
                  w_in[l], ssm_A_re[l], ssm_A_im[l], ssm_log_dt[l], ssm_B_re[l], ssm_B_im[l],
                  ssm_C_re[l], ssm_C_im[l], ssm_D[l], ssm_w_glu[l], ssm_b_glu[l], attn_sinks[l],
                  ssm_out_norm[l], attn_out_norm[l], w_out[l], ffn2_norm[l], ffn2_w_gate[l],
                  ffn2_w_up[l], ffn2_w_down[l], final_norm)
```

```python
import functools
import math

import jax
import jax.numpy as jnp
from jax import lax
from jax.experimental import pallas as pl
from jax.experimental.pallas import tpu as pltpu

F32 = jnp.float32
BF16 = jnp.bfloat16

SSM_GROUP = 16
SSM_STATE = 64
HEAD_DIM = 64
N_KV_HEADS = 2
WINDOW = 128
ROPE_DIM = HEAD_DIM // 4
ROPE_THETA = 500000.0
RES_HALF = 0.5
EPS = 1e-6
NEG_INF = -1e30
LOG2E = math.log2(math.e)
LANES = 128
SUBLANES = 8
MXU_N = 256

VMEM_LIMIT = 56 * 1024 * 1024


def _rms(x, g):
    return x * lax.rsqrt(jnp.mean(x * x, axis=-1, keepdims=True) + EPS) * g


def _swiglu_residual(x, g_ref, wg_ref, wu_ref, wd_ref, fc):
    hn = _rms(x, g_ref[...]).astype(BF16)
    acc = x
    d_ff = wd_ref.shape[0]
    chunks = [slice(lo, min(lo + fc, d_ff)) for lo in range(0, d_ff, fc)]

    def gate_up(sl):
        return (jnp.dot(hn, wg_ref[:, sl], preferred_element_type=F32),
                jnp.dot(hn, wu_ref[:, sl], preferred_element_type=F32))

    nxt = gate_up(chunks[0])
    for i, sl in enumerate(chunks):
        g, u = nxt
        if i + 1 < len(chunks):
            nxt = gate_up(chunks[i + 1])
        a = (g * jax.nn.sigmoid(g) * u).astype(BF16)
        acc = acc + RES_HALF * jnp.dot(a, wd_ref[sl, :], preferred_element_type=F32)
    return acc


def _rope(t, c, s1, s2):
    half = ROPE_DIM // 2
    return t * c + pltpu.roll(t, half, 1) * s1 + pltpu.roll(t, LANES - half, 1) * s2


def _placed_kv(t):
    lane = lax.broadcasted_iota(jnp.int32, (1, LANES), 1)
    head0 = jnp.where(lane < HEAD_DIM, t, 0.0)
    head1 = jnp.where(lane < HEAD_DIM, 0.0, t)
    blocks = [head0, pltpu.roll(head0, HEAD_DIM, 1), pltpu.roll(head1, HEAD_DIM, 1), head1]
    return jnp.concatenate(blocks, axis=-1).astype(BF16)


def _ffn_in_kernel(x_ref, g1_ref, wg_ref, wu_ref, wd_ref, gm_ref, win_ref,
                   rc_ref, rs1_ref, rs2_ref,
                   x1_ref, u_ref, q_ref, k_ref, v_ref, *, fc, d_ssm, d_attn, d_kv):
    x1 = _swiglu_residual(x_ref[...], g1_ref, wg_ref, wu_ref, wd_ref, fc)
    x1_ref[...] = x1
    hn = _rms(x1, gm_ref[...]).astype(BF16)
    c, s1, s2 = rc_ref[...], rs1_ref[...], rs2_ref[...]
    half = x_ref.shape[0] // 2
    for rs in (slice(0, half), slice(half, 2 * half)):
        kv = jnp.dot(hn[rs], win_ref[:, d_ssm + d_attn:], preferred_element_type=F32)
        k_ref[rs, :] = _placed_kv(_rope(kv[:, :d_kv], c[rs], s1[rs], s2[rs]))
        v_ref[rs, :] = _placed_kv(kv[:, d_kv:])
    q = jnp.dot(hn, win_ref[:, d_ssm:d_ssm + d_attn], preferred_element_type=F32)
    for j in range(d_attn // LANES):
        sl = slice(j * LANES, (j + 1) * LANES)
        q_ref[:, sl] = _rope(q[:, sl], c, s1, s2).astype(BF16)
    u_ref[...] = jnp.dot(hn, win_ref[:, :d_ssm], preferred_element_type=F32)


def _ssm_kernel(u_ref, are_ref, aim_ref, bre_ref, bim_ref, cre_ref, cim_ref,
                d_ref, wglu_ref, bglu_ref, gn_ref,
                y_ref, xre_ref, xim_ref, tb_ref, *s_refs, tt, n_chain):
    sre, sim = s_refs[:n_chain], s_refs[n_chain:]
    rows = tt * SUBLANES
    d_ssm = d_ref.shape[-1]
    n_lane_blk = d_ssm // LANES
    cw = d_ssm // n_chain
    sw = are_ref.shape[-1] // n_chain

    @pl.when(pl.program_id(0) == 0)
    def _():
        xre_ref[...] = jnp.zeros_like(xre_ref)
        xim_ref[...] = jnp.zeros_like(xim_ref)

    for b in range(SUBLANES):
        for k in range(n_lane_blk):
            lo = b * d_ssm + k * LANES
            tb_ref[k, pl.ds(b, tt, stride=SUBLANES), :] = u_ref[:, lo:lo + LANES]
    u = jnp.concatenate([tb_ref[k] for k in range(n_lane_blk)], axis=-1)
    ub = u.astype(BF16)
    for c in range(n_chain):
        uc = ub[:, c * cw:(c + 1) * cw]
        sre[c][...] = jnp.dot(uc, bre_ref[c], preferred_element_type=F32)
        sim[c][...] = jnp.dot(uc, bim_ref[c], preferred_element_type=F32)

    ys = []
    for c in range(n_chain):
        st = slice(c * sw, (c + 1) * sw)
        ar, ai = are_ref[:, st], aim_ref[:, st]
        xr, xi = xre_ref[:, st], xim_ref[:, st]
        for t in range(tt):
            r = slice(t * SUBLANES, (t + 1) * SUBLANES)
            xr, xi = (ar * xr - ai * xi + sre[c][r, :],
                      ar * xi + ai * xr + sim[c][r, :])
            sre[c][r, :] = xr
            sim[c][r, :] = xi
        xre_ref[:, st] = xr
        xim_ref[:, st] = xi
        ys.append(jnp.dot(sre[c][...].astype(BF16), cre_ref[c], preferred_element_type=F32)
                  + jnp.dot(sim[c][...].astype(BF16), cim_ref[c], preferred_element_type=F32))
    y = jnp.concatenate(ys, axis=-1) + d_ref[...] * u
    y = jax.nn.gelu(y)
    z = jnp.dot(y.astype(BF16), wglu_ref[...], preferred_element_type=F32) + bglu_ref[...]
    y = y * jax.nn.sigmoid(z)
    y = _rms(y, gn_ref[...])
    for k in range(n_lane_blk):
        tb_ref[k] = y[:, k * LANES:(k + 1) * LANES]
    for b in range(SUBLANES):
        for k in range(n_lane_blk):
            lo = b * d_ssm + k * LANES
            y_ref[:, lo:lo + LANES] = tb_ref[k, pl.ds(b, tt, stride=SUBLANES), :]


def _attn_kernel(sink_ref, q_ref, kc_ref, kp_ref, vc_ref, vp_ref, gn_ref, o_ref, *, nq):
    blk = WINDOW
    n = pl.program_id(1)
    lane = lax.broadcasted_iota(jnp.int32, (1, LANES), 1)
    k_all = jnp.concatenate([kp_ref[...], kc_ref[...]], axis=0)
    v_all = jnp.concatenate([vp_ref[...], vc_ref[...]], axis=0)

    bf_tile = 2 * SUBLANES
    row0 = lax.broadcasted_iota(jnp.int32, (bf_tile, 1), 0) == 0
    kmat, vext, vhead = [], [], []
    for kh in range(N_KV_HEADS):
        kmat.append([]), vext.append([]), vhead.append([])
        for half in range(2):
            sl = slice((kh * 2 + half) * LANES, (kh * 2 + half + 1) * LANES)
            ones = jnp.broadcast_to(jnp.where((lane // HEAD_DIM) == half, 1.0, 0.0),
                                    (v_all.shape[0], LANES)).astype(BF16)
            kmat[kh].append(k_all[:, sl])
            vext[kh].append(jnp.concatenate([v_all[:, sl], ones], axis=-1))
            vhead[kh].append([
                jnp.concatenate(
                    [jnp.where(row0, 0.0, v_all[j * blk:j * blk + bf_tile, sl].astype(F32)).astype(BF16),
                     ones[:bf_tile]], axis=-1)
                for j in range(nq)])

    qi = lax.broadcasted_iota(jnp.int32, (blk, 2 * blk), 0)
    sj = lax.broadcasted_iota(jnp.int32, (blk, 2 * blk), 1)
    band = (sj > qi) & (sj <= qi + blk)
    band_first = band & ((sj >= blk) | (n > 0))
    col0 = sj == 0
    n_pairs = q_ref.shape[-1] // LANES
    heads_per_kv = (2 * n_pairs) // N_KV_HEADS

    outs = []
    for j in range(nq):
        valid = band_first if j == 0 else band
        pair_outs = []
        for pr in range(n_pairs):
            qp = q_ref[j * blk:(j + 1) * blk, pr * LANES:(pr + 1) * LANES]
            res = None
            for half in range(2):
                h = 2 * pr + half
                kh = h // heads_per_kv
                fill = jnp.where(col0, sink_ref[h] * LOG2E, NEG_INF)
                s = lax.dot_general(qp, kmat[kh][half][j * blk:(j + 2) * blk],
                                    (((1,), (1,)), ((), ())), preferred_element_type=F32)
                s = jnp.where(valid, s, fill)
                m = jnp.max(s, axis=-1, keepdims=True)
                p = jnp.exp2(s - m).astype(BF16)
                win = jnp.concatenate([vhead[kh][half][j],
                                       vext[kh][half][j * blk + bf_tile:(j + 2) * blk]], axis=0)
                o = jnp.dot(p, win, preferred_element_type=F32)
                res = o if res is None else res + o
            pair_outs.append(res[:, :LANES] / res[:, LANES:])
        outs.append(jnp.concatenate(pair_outs, axis=-1))
    y = jnp.concatenate(outs, axis=0)
    o_ref[...] = _rms(y, gn_ref[...]).astype(o_ref.dtype)


def _out_ffn_kernel(x1_ref, ys_ref, ya_ref, wo_ref, g2_ref, wg_ref, wu_ref, wd_ref, gf_ref,
                    o_ref, *, fc):
    d_ssm = ys_ref.shape[-1]
    x2 = (x1_ref[...]
          + jnp.dot(ys_ref[...].astype(BF16), wo_ref[:d_ssm, :], preferred_element_type=F32)
          + jnp.dot(ya_ref[...], wo_ref[d_ssm:, :], preferred_element_type=F32))
    x3 = _swiglu_residual(x2, g2_ref, wg_ref, wu_ref, wd_ref, fc)
    o_ref[...] = _rms(x3, gf_ref[...])


def _const_spec(shape):
    nd = len(shape)
    return pl.BlockSpec(shape, lambda *_: (0,) * nd, pipeline_mode=pl.Buffered(1))


def _rope_lane_tables(seq):
    half = ROPE_DIM // 2
    inv_freq = ROPE_THETA ** (-jnp.arange(half, dtype=F32) * 2.0 / ROPE_DIM)
    ang = jnp.arange(seq, dtype=F32)[:, None] * inv_freq[None, :]
    cos, sin = jnp.cos(ang), jnp.sin(ang)
    ones = jnp.ones((seq, HEAD_DIM - ROPE_DIM), F32)
    zeros = jnp.zeros((seq, HEAD_DIM - ROPE_DIM), F32)
    zh = jnp.zeros((seq, half), F32)
    c = jnp.concatenate([cos, cos, ones], axis=-1)
    s1 = jnp.concatenate([zh, sin, zeros], axis=-1)
    s2 = jnp.concatenate([-sin, zh, zeros], axis=-1)
    reps = LANES // HEAD_DIM
    return tuple(jnp.tile(t, (1, reps)) for t in (c, s1, s2))


def _block_diag(m, n_halves):
    g, r, c = m.shape
    gh = g // n_halves
    m = m.reshape(n_halves, gh, r, c)
    eye = jnp.eye(gh, dtype=m.dtype)
    return jnp.einsum('hgrc,gk->hgrkc', m, eye).reshape(n_halves, gh * r, gh * c)


def _discretize_kernel(are_ref, aim_ref, ldt_ref, bre_ref, bim_ref,
                       lre_ref, lim_ref, bbre_ref, bbim_ref):
    a_re, a_im = are_ref[...], aim_ref[...]
    dt = jnp.exp(ldt_ref[...])
    mag = jnp.exp(a_re * dt)
    l_re = mag * jnp.cos(a_im * dt)
    l_im = mag * jnp.sin(a_im * dt)
    lre_ref[...] = l_re
    lim_ref[...] = l_im
    n_re = l_re - 1.0
    den = a_re * a_re + a_im * a_im
    z_re = ((n_re * a_re + l_im * a_im) / den)[:, None, :]
    z_im = ((l_im * a_re - n_re * a_im) / den)[:, None, :]
    b_re, b_im = bre_ref[...], bim_ref[...]
    bbre_ref[...] = z_re * b_re - z_im * b_im
    bbim_ref[...] = z_re * b_im + z_im * b_re


def _ssm_params(a_re, a_im, log_dt, b_re, b_im, c_re, c_im, n_chain):
    g, p, c = b_re.shape
    lam_re, lam_im, bb_re, bb_im = pl.pallas_call(
        _discretize_kernel,
        out_shape=[jax.ShapeDtypeStruct((g, p), F32)] * 2 + [jax.ShapeDtypeStruct((g, c, p), F32)] * 2,
        name="ssm_discretize",
    )(a_re, a_im, log_dt.reshape(g, 1), jnp.swapaxes(b_re, 1, 2), jnp.swapaxes(b_im, 1, 2))
    n_state = g * p
    are = jnp.broadcast_to(lam_re.reshape(1, n_state), (SUBLANES, n_state))
    aim = jnp.broadcast_to(lam_im.reshape(1, n_state), (SUBLANES, n_state))
    bre = _block_diag(bb_re, n_chain).astype(BF16)
    bim = _block_diag(bb_im, n_chain).astype(BF16)
    cre = _block_diag(jnp.swapaxes(c_re, 1, 2), n_chain).astype(BF16)
    cim = _block_diag(jnp.swapaxes(-c_im, 1, 2), n_chain).astype(BF16)
    return are, aim, bre, bim, cre, cim


def _layer(x, cs, ffn1_norm, wg1, wu1, wd1, mix_norm, w_in, ssm_A_re, ssm_A_im, ssm_log_dt,
           ssm_B_re, ssm_B_im, ssm_C_re, ssm_C_im, ssm_D, ssm_w_glu, ssm_b_glu, attn_sinks,
           ssm_out_norm, attn_out_norm, w_out, ffn2_norm, wg2, wu2, wd2, final_norm):
    bsz, seq, d_model = x.shape
    d_ff = wg1.shape[1]
    d_ssm = ssm_D.shape[0]
    d_attn = attn_out_norm.shape[0]
    d_kv = N_KV_HEADS * HEAD_DIM
    d_kvp = 2 * N_KV_HEADS * LANES
    assert d_kv == LANES
    d_in = w_in.shape[1]
    tm = 1024
    fc = MXU_N
    tt = 128
    n_chain = 4
    nq = 16
    row = lambda v: v.reshape(1, -1).astype(F32)

    col_scale = jnp.concatenate([jnp.ones((d_ssm,), F32),
                                 jnp.full((d_attn,), LOG2E / math.sqrt(HEAD_DIM), F32),
                                 jnp.ones((2 * d_kv,), F32)])
    w_in_b = (w_in * col_scale[None, :]).astype(BF16)

    cparams = lambda sem: pltpu.CompilerParams(dimension_semantics=sem, vmem_limit_bytes=VMEM_LIMIT)

    tile = lambda w: pl.BlockSpec((None, tm, w), lambda b, i: (b, i, 0))
    rope_spec = pl.BlockSpec((tm, LANES), lambda b, i: (i, 0))
    tm_tile = pl.BlockSpec((tm, d_ssm), lambda b, i: (i, b))
    x1, u_tb, q, k, v = pl.pallas_call(
        functools.partial(_ffn_in_kernel, fc=fc, d_ssm=d_ssm, d_attn=d_attn, d_kv=d_kv),
        grid=(bsz, seq // tm),
        in_specs=[tile(d_model), _const_spec((1, d_model)), _const_spec((d_model, d_ff)),
                  _const_spec((d_model, d_ff)), _const_spec((d_ff, d_model)),
                  _const_spec((1, d_model)), _const_spec((d_model, d_in)),
                  rope_spec, rope_spec, rope_spec],
        out_specs=[tile(d_model),
                   tm_tile,
                   tile(d_attn), tile(d_kvp), tile(d_kvp)],
        out_shape=[jax.ShapeDtypeStruct((bsz, seq, d_model), F32),
                   jax.ShapeDtypeStruct((seq, bsz * d_ssm), F32),
                   jax.ShapeDtypeStruct((bsz, seq, d_attn), BF16),
                   jax.ShapeDtypeStruct((bsz, seq, d_kvp), BF16),
                   jax.ShapeDtypeStruct((bsz, seq, d_kvp), BF16)],
        compiler_params=cparams(("parallel", "parallel")),
        name="ffn_in",
    )(x, row(ffn1_norm), wg1.astype(BF16), wu1.astype(BF16), wd1.astype(BF16),
      row(mix_norm), w_in_b, *cs)

    assert bsz == SUBLANES
    are, aim, bre, bim, cre, cim = _ssm_params(ssm_A_re, ssm_A_im, ssm_log_dt,
                                               ssm_B_re, ssm_B_im, ssm_C_re, ssm_C_im, n_chain)
    n_state = are.shape[-1]
    tb_spec = pl.BlockSpec((tt, bsz * d_ssm), lambda i: (i, 0))
    y_ssm_tb = pl.pallas_call(
        functools.partial(_ssm_kernel, tt=tt, n_chain=n_chain),
        grid=(seq // tt,),
        in_specs=[tb_spec, _const_spec(are.shape), _const_spec(aim.shape),
                  _const_spec(bre.shape), _const_spec(bim.shape),
                  _const_spec(cre.shape), _const_spec(cim.shape),
                  _const_spec((1, d_ssm)), _const_spec((d_ssm, d_ssm)),
                  _const_spec((1, d_ssm)), _const_spec((1, d_ssm))],
        out_specs=tb_spec,
        out_shape=jax.ShapeDtypeStruct((seq, bsz * d_ssm), F32),
        scratch_shapes=([pltpu.VMEM((bsz, n_state), F32)] * 2
                        + [pltpu.VMEM((d_ssm // LANES, tt * bsz, LANES), F32)]
                        + [pltpu.VMEM((tt * bsz, n_state // n_chain), F32)] * (2 * n_chain)),
        compiler_params=cparams(("arbitrary",)),
        name="ssm",
    )(u_tb, are, aim, bre, bim, cre, cim,
      row(ssm_D), ssm_w_glu.astype(BF16), row(ssm_b_glu), row(ssm_out_norm))

    tq = nq * WINDOW
    cur = lambda w: pl.BlockSpec((None, tq, w), lambda b, n: (b, n, 0))
    prev = lambda w: pl.BlockSpec((None, WINDOW, w), lambda b, n: (b, jnp.maximum(n * nq - 1, 0), 0))
    y_attn = pl.pallas_call(
        functools.partial(_attn_kernel, nq=nq),
        grid=(bsz, seq // tq),
        in_specs=[pl.BlockSpec(memory_space=pltpu.SMEM),
                  cur(d_attn), cur(d_kvp), prev(d_kvp), cur(d_kvp), prev(d_kvp),
                  pl.BlockSpec((1, d_attn), lambda b, n: (0, 0))],
        out_specs=cur(d_attn),
        out_shape=jax.ShapeDtypeStruct((bsz, seq, d_attn), BF16),
        compiler_params=cparams(("parallel", "parallel")),
        name="attn",
    )(attn_sinks.astype(F32), q, k, k, v, v, row(attn_out_norm))

    out = pl.pallas_call(
        functools.partial(_out_ffn_kernel, fc=fc),
        grid=(bsz, seq // tm),
        in_specs=[tile(d_model),
                  tm_tile,
                  tile(d_attn), _const_spec((d_ssm + d_attn, d_model)),
                  _const_spec((1, d_model)), _const_spec((d_model, d_ff)),
                  _const_spec((d_model, d_ff)), _const_spec((d_ff, d_model)),
                  _const_spec((1, d_model))],
        out_specs=tile(d_model),
        out_shape=jax.ShapeDtypeStruct((bsz, seq, d_model), F32),
        compiler_params=cparams(("parallel", "parallel")),
        name="out_ffn",
    )(x1, y_ssm_tb, y_attn, w_out.astype(BF16),
      row(ffn2_norm), wg2.astype(BF16), wu2.astype(BF16), wd2.astype(BF16), row(final_norm))
    return out


def kernel(x, ffn1_norm, ffn1_w_gate, ffn1_w_up, ffn1_w_down, mix_norm, w_in, ssm_A_re, ssm_A_im, ssm_log_dt, ssm_B_re, ssm_B_im, ssm_C_re, ssm_C_im, ssm_D, ssm_w_glu, ssm_b_glu, attn_sinks, ssm_out_norm, attn_out_norm, w_out, ffn2_norm, ffn2_w_gate, ffn2_w_up, ffn2_w_down, final_norm):
    depth = ffn1_norm.shape[0]
    assert depth == 1, "the final norm is fused into the last layer's kernel"
    cs = _rope_lane_tables(x.shape[1])
    l = 0
    return _layer(x, cs, ffn1_norm[l], ffn1_w_gate[l], ffn1_w_up[l], ffn1_w_down[l], mix_norm[l],
                  w_in[l], ssm_A_re[l], ssm_A_im[l], ssm_log_dt[l], ssm_B_re[l], ssm_B_im[l],
                  ssm_C_re[l], ssm_C_im[l], ssm_D[l], ssm_w_glu[l], ssm_b_glu[l], attn_sinks[l],
                  ssm_out_norm[l], attn_out_norm[l], w_out[l], ffn2_norm[l], ffn2_w_gate[l],
                  ffn2_w_up[l], ffn2_w_down[l], final_norm)
```

```python
import functools
import math

import jax
import jax.numpy as jnp
from jax import lax
from jax.experimental import pallas as pl
from jax.experimental.pallas import tpu as pltpu

F32 = jnp.float32
BF16 = jnp.bfloat16

SSM_GROUP = 16
SSM_STATE = 64
HEAD_DIM = 64
N_KV_HEADS = 2
WINDOW = 128
ROPE_DIM = HEAD_DIM // 4
ROPE_THETA = 500000.0
RES_HALF = 0.5
EPS = 1e-6
NEG_INF = -1e30
LOG2E = math.log2(math.e)
LANES = 128
SUBLANES = 8
MXU_N = 256

VMEM_LIMIT = 56 * 1024 * 1024


def _rms(x, g):
    return x * lax.rsqrt(jnp.mean(x * x, axis=-1, keepdims=True) + EPS) * g


def _swiglu_residual(x, g_ref, wg_ref, wu_ref, wd_ref, fc):
    hn = _rms(x, g_ref[...]).astype(BF16)
    acc = x
    d_ff = wd_ref.shape[0]
    chunks = [slice(lo, min(lo + fc, d_ff)) for lo in range(0, d_ff, fc)]

    def gate_up(sl):
        return (jnp.dot(hn, wg_ref[:, sl], preferred_element_type=F32),
                jnp.dot(hn, wu_ref[:, sl], preferred_element_type=F32))

    nxt = gate_up(chunks[0])
    for i, sl in enumerate(chunks):
        g, u = nxt
        if i + 1 < len(chunks):
            nxt = gate_up(chunks[i + 1])
        a = (g * jax.nn.sigmoid(g) * u).astype(BF16)
        acc = acc + RES_HALF * jnp.dot(a, wd_ref[sl, :], preferred_element_type=F32)
    return acc


def _rope(t, c, s1, s2):
    half = ROPE_DIM // 2
    return t * c + pltpu.roll(t, half, 1) * s1 + pltpu.roll(t, LANES - half, 1) * s2


def _placed_kv(t):
    lane = lax.broadcasted_iota(jnp.int32, (1, LANES), 1)
    head0 = jnp.where(lane < HEAD_DIM, t, 0.0)
    head1 = jnp.where(lane < HEAD_DIM, 0.0, t)
    blocks = [head0, pltpu.roll(head0, HEAD_DIM, 1), pltpu.roll(head1, HEAD_DIM, 1), head1]
    return jnp.concatenate(blocks, axis=-1).astype(BF16)


def _ffn_in_kernel(x_ref, g1_ref, wg_ref, wu_ref, wd_ref, gm_ref, win_ref,
                   rope_ref,
                   x1_ref, u_ref, qkv_ref, *, fc, d_ssm, d_attn, d_kv):
    x1 = _swiglu_residual(x_ref[...], g1_ref, wg_ref, wu_ref, wd_ref, fc)
    x1_ref[...] = x1
    hn = _rms(x1, gm_ref[...]).astype(BF16)
    c, s1, s2 = (rope_ref[:, j * LANES:(j + 1) * LANES] for j in range(3))
    d_kvp = (qkv_ref.shape[-1] - d_attn) // 2
    half = x_ref.shape[0] // 2
    for rs in (slice(0, half), slice(half, 2 * half)):
        kv = jnp.dot(hn[rs], win_ref[:, d_ssm + d_attn:], preferred_element_type=F32)
        qkv_ref[rs, d_attn:d_attn + d_kvp] = _placed_kv(_rope(kv[:, :d_kv], c[rs], s1[rs], s2[rs]))
        qkv_ref[rs, d_attn + d_kvp:] = _placed_kv(kv[:, d_kv:])
    q = jnp.dot(hn, win_ref[:, d_ssm:d_ssm + d_attn], preferred_element_type=F32)
    for j in range(d_attn // LANES):
        sl = slice(j * LANES, (j + 1) * LANES)
        qkv_ref[:, sl] = _rope(q[:, sl], c, s1, s2).astype(BF16)
    u_ref[...] = jnp.dot(hn, win_ref[:, :d_ssm], preferred_element_type=F32)


def _ssm_kernel(u_ref, are_ref, aim_ref, bre_ref, bim_ref, cre_ref, cim_ref,
                d_ref, wglu_ref, bglu_ref, gn_ref,
                y_ref, xre_ref, xim_ref, tb_ref, *s_refs, tt, n_chain):
    sre, sim = s_refs[:n_chain], s_refs[n_chain:]
    rows = tt * SUBLANES
    d_ssm = d_ref.shape[-1]
    n_lane_blk = d_ssm // LANES
    cw = d_ssm // n_chain
    sw = are_ref.shape[-1] // n_chain

    @pl.when(pl.program_id(0) == 0)
    def _():
        xre_ref[...] = jnp.zeros_like(xre_ref)
        xim_ref[...] = jnp.zeros_like(xim_ref)

    for b in range(SUBLANES):
        for k in range(n_lane_blk):
            lo = b * d_ssm + k * LANES
            tb_ref[k, pl.ds(b, tt, stride=SUBLANES), :] = u_ref[:, lo:lo + LANES]
    u = jnp.concatenate([tb_ref[k] for k in range(n_lane_blk)], axis=-1)
    ub = u.astype(BF16)
    for c in range(n_chain):
        uc = ub[:, c * cw:(c + 1) * cw]
        sre[c][...] = jnp.dot(uc, bre_ref[c], preferred_element_type=F32)
        sim[c][...] = jnp.dot(uc, bim_ref[c], preferred_element_type=F32)

    ys = []
    for c in range(n_chain):
        st = slice(c * sw, (c + 1) * sw)
        ar, ai = are_ref[:, st], aim_ref[:, st]
        xr, xi = xre_ref[:, st], xim_ref[:, st]
        for t in range(tt):
            r = slice(t * SUBLANES, (t + 1) * SUBLANES)
            xr, xi = (ar * xr - ai * xi + sre[c][r, :],
                      ar * xi + ai * xr + sim[c][r, :])
            sre[c][r, :] = xr
            sim[c][r, :] = xi
        xre_ref[:, st] = xr
        xim_ref[:, st] = xi
        ys.append(jnp.dot(sre[c][...].astype(BF16), cre_ref[c], preferred_element_type=F32)
                  + jnp.dot(sim[c][...].astype(BF16), cim_ref[c], preferred_element_type=F32))
    y = jnp.concatenate(ys, axis=-1) + d_ref[...] * u
    y = jax.nn.gelu(y)
    z = jnp.dot(y.astype(BF16), wglu_ref[...], preferred_element_type=F32) + bglu_ref[...]
    y = y * jax.nn.sigmoid(z)
    y = _rms(y, gn_ref[...])
    for k in range(n_lane_blk):
        tb_ref[k] = y[:, k * LANES:(k + 1) * LANES]
    for b in range(SUBLANES):
        for k in range(n_lane_blk):
            lo = b * d_ssm + k * LANES
            y_ref[:, lo:lo + LANES] = tb_ref[k, pl.ds(b, tt, stride=SUBLANES), :]


def _attn_kernel(sink_ref, q_ref, kc_ref, kp_ref, vc_ref, vp_ref, gn_ref, o_ref, *, nq):
    blk = WINDOW
    n = pl.program_id(1)
    lane = lax.broadcasted_iota(jnp.int32, (1, LANES), 1)
    k_all = jnp.concatenate([kp_ref[...], kc_ref[...]], axis=0)
    v_all = jnp.concatenate([vp_ref[...], vc_ref[...]], axis=0)

    bf_tile = 2 * SUBLANES
    row0 = lax.broadcasted_iota(jnp.int32, (bf_tile, 1), 0) == 0
    kmat, vext, vhead = [], [], []
    for kh in range(N_KV_HEADS):
        kmat.append([]), vext.append([]), vhead.append([])
        for half in range(2):
            sl = slice((kh * 2 + half) * LANES, (kh * 2 + half + 1) * LANES)
            ones = jnp.broadcast_to(jnp.where((lane // HEAD_DIM) == half, 1.0, 0.0),
                                    (v_all.shape[0], LANES)).astype(BF16)
            kmat[kh].append(k_all[:, sl])
            vext[kh].append(jnp.concatenate([v_all[:, sl], ones], axis=-1))
            vhead[kh].append([
                jnp.concatenate(
                    [jnp.where(row0, 0.0, v_all[j * blk:j * blk + bf_tile, sl].astype(F32)).astype(BF16),
                     ones[:bf_tile]], axis=-1)
                for j in range(nq)])

    qi = lax.broadcasted_iota(jnp.int32, (blk, 2 * blk), 0)
    sj = lax.broadcasted_iota(jnp.int32, (blk, 2 * blk), 1)
    band = (sj > qi) & (sj <= qi + blk)
    band_first = band & ((sj >= blk) | (n > 0))
    col0 = sj == 0
    n_pairs = q_ref.shape[-1] // LANES
    heads_per_kv = (2 * n_pairs) // N_KV_HEADS

    outs = []
    for j in range(nq):
        valid = band_first if j == 0 else band
        pair_outs = []
        for pr in range(n_pairs):
            qp = q_ref[j * blk:(j + 1) * blk, pr * LANES:(pr + 1) * LANES]
            res = None
            for half in range(2):
                h = 2 * pr + half
                kh = h // heads_per_kv
                fill = jnp.where(col0, sink_ref[h] * LOG2E, NEG_INF)
                s = lax.dot_general(qp, kmat[kh][half][j * blk:(j + 2) * blk],
                                    (((1,), (1,)), ((), ())), preferred_element_type=F32)
                s = jnp.where(valid, s, fill)
                m = jnp.max(s, axis=-1, keepdims=True)
                p = jnp.exp2(s - m).astype(BF16)
                win = jnp.concatenate([vhead[kh][half][j],
                                       vext[kh][half][j * blk + bf_tile:(j + 2) * blk]], axis=0)
                o = jnp.dot(p, win, preferred_element_type=F32)
                res = o if res is None else res + o
            pair_outs.append(res[:, :LANES] / res[:, LANES:])
        outs.append(jnp.concatenate(pair_outs, axis=-1))
    y = jnp.concatenate(outs, axis=0)
    o_ref[...] = _rms(y, gn_ref[...]).astype(o_ref.dtype)


def _out_ffn_kernel(x1_ref, ys_ref, ya_ref, wo_ref, g2_ref, wg_ref, wu_ref, wd_ref, gf_ref,
                    o_ref, *, fc):
    d_ssm = ys_ref.shape[-1]
    x2 = (x1_ref[...]
          + jnp.dot(ys_ref[...].astype(BF16), wo_ref[:d_ssm, :], preferred_element_type=F32)
          + jnp.dot(ya_ref[...], wo_ref[d_ssm:, :], preferred_element_type=F32))
    x3 = _swiglu_residual(x2, g2_ref, wg_ref, wu_ref, wd_ref, fc)
    o_ref[...] = _rms(x3, gf_ref[...])


def _const_spec(shape):
    nd = len(shape)
    return pl.BlockSpec(shape, lambda *_: (0,) * nd, pipeline_mode=pl.Buffered(1))


def _rope_lane_tables(seq):
    half = ROPE_DIM // 2
    inv_freq = ROPE_THETA ** (-jnp.arange(half, dtype=F32) * 2.0 / ROPE_DIM)
    ang = jnp.arange(seq, dtype=F32)[:, None] * inv_freq[None, :]
    cos, sin = jnp.cos(ang), jnp.sin(ang)
    ones = jnp.ones((seq, HEAD_DIM - ROPE_DIM), F32)
    zeros = jnp.zeros((seq, HEAD_DIM - ROPE_DIM), F32)
    zh = jnp.zeros((seq, half), F32)
    c = jnp.concatenate([cos, cos, ones], axis=-1)
    s1 = jnp.concatenate([zh, sin, zeros], axis=-1)
    s2 = jnp.concatenate([-sin, zh, zeros], axis=-1)
    reps = LANES // HEAD_DIM
    return tuple(jnp.tile(t, (1, reps)) for t in (c, s1, s2))


def _block_diag(m, n_halves):
    g, r, c = m.shape
    gh = g // n_halves
    m = m.reshape(n_halves, gh, r, c)
    eye = jnp.eye(gh, dtype=m.dtype)
    return jnp.einsum('hgrc,gk->hgrkc', m, eye).reshape(n_halves, gh * r, gh * c)


def _discretize_kernel(are_ref, aim_ref, ldt_ref, bre_ref, bim_ref,
                       lre_ref, lim_ref, bbre_ref, bbim_ref):
    a_re, a_im = are_ref[...], aim_ref[...]
    dt = jnp.exp(ldt_ref[...])
    mag = jnp.exp(a_re * dt)
    l_re = mag * jnp.cos(a_im * dt)
    l_im = mag * jnp.sin(a_im * dt)
    lre_ref[...] = l_re
    lim_ref[...] = l_im
    n_re = l_re - 1.0
    den = a_re * a_re + a_im * a_im
    z_re = ((n_re * a_re + l_im * a_im) / den)[:, None, :]
    z_im = ((l_im * a_re - n_re * a_im) / den)[:, None, :]
    b_re, b_im = bre_ref[...], bim_ref[...]
    bbre_ref[...] = z_re * b_re - z_im * b_im
    bbim_ref[...] = z_re * b_im + z_im * b_re


def _ssm_params(a_re, a_im, log_dt, b_re, b_im, c_re, c_im, n_chain):
    g, p, c = b_re.shape
    lam_re, lam_im, bb_re, bb_im = pl.pallas_call(
        _discretize_kernel,
        out_shape=[jax.ShapeDtypeStruct((g, p), F32)] * 2 + [jax.ShapeDtypeStruct((g, c, p), F32)] * 2,
        name="ssm_discretize",
    )(a_re, a_im, log_dt.reshape(g, 1), jnp.swapaxes(b_re, 1, 2), jnp.swapaxes(b_im, 1, 2))
    n_state = g * p
    are = jnp.broadcast_to(lam_re.reshape(1, n_state), (SUBLANES, n_state))
    aim = jnp.broadcast_to(lam_im.reshape(1, n_state), (SUBLANES, n_state))
    bre = _block_diag(bb_re, n_chain).astype(BF16)
    bim = _block_diag(bb_im, n_chain).astype(BF16)
    cre = _block_diag(jnp.swapaxes(c_re, 1, 2), n_chain).astype(BF16)
    cim = _block_diag(jnp.swapaxes(-c_im, 1, 2), n_chain).astype(BF16)
    return are, aim, bre, bim, cre, cim


def _layer(x, cs, ffn1_norm, wg1, wu1, wd1, mix_norm, w_in, ssm_A_re, ssm_A_im, ssm_log_dt,
           ssm_B_re, ssm_B_im, ssm_C_re, ssm_C_im, ssm_D, ssm_w_glu, ssm_b_glu, attn_sinks,
           ssm_out_norm, attn_out_norm, w_out, ffn2_norm, wg2, wu2, wd2, final_norm):
    bsz, seq, d_model = x.shape
    d_ff = wg1.shape[1]
    d_ssm = ssm_D.shape[0]
    d_attn = attn_out_norm.shape[0]
    d_kv = N_KV_HEADS * HEAD_DIM
    d_kvp = 2 * N_KV_HEADS * LANES
    assert d_kv == LANES
    d_in = w_in.shape[1]
    tm = 1024
    fc = MXU_N
    tt = 128
    n_chain = 4
    nq = 16
    row = lambda v: v.reshape(1, -1).astype(F32)

    col_scale = jnp.concatenate([jnp.ones((d_ssm,), F32),
                                 jnp.full((d_attn,), LOG2E / math.sqrt(HEAD_DIM), F32),
                                 jnp.ones((2 * d_kv,), F32)])
    w_in_b = (w_in * col_scale[None, :]).astype(BF16)

    cparams = lambda sem: pltpu.CompilerParams(dimension_semantics=sem, vmem_limit_bytes=VMEM_LIMIT)

    tile = lambda w: pl.BlockSpec((None, tm, w), lambda b, i: (b, i, 0))
    rope = jnp.concatenate(cs, axis=-1)
    tm_tile = pl.BlockSpec((tm, d_ssm), lambda b, i: (i, b))
    assert d_attn == d_kvp
    x1, u_tb, qkv = pl.pallas_call(
        functools.partial(_ffn_in_kernel, fc=fc, d_ssm=d_ssm, d_attn=d_attn, d_kv=d_kv),
        grid=(bsz, seq // tm),
        in_specs=[tile(d_model), _const_spec((1, d_model)), _const_spec((d_model, d_ff)),
                  _const_spec((d_model, d_ff)), _const_spec((d_ff, d_model)),
                  _const_spec((1, d_model)), _const_spec((d_model, d_in)),
                  pl.BlockSpec((tm, 3 * LANES), lambda b, i: (i, 0))],
        out_specs=[tile(d_model),
                   tm_tile,
                   tile(d_attn + 2 * d_kvp)],
        out_shape=[jax.ShapeDtypeStruct((bsz, seq, d_model), F32),
                   jax.ShapeDtypeStruct((seq, bsz * d_ssm), F32),
                   jax.ShapeDtypeStruct((bsz, seq, d_attn + 2 * d_kvp), BF16)],
        compiler_params=cparams(("parallel", "parallel")),
        name="ffn_in",
    )(x, row(ffn1_norm), wg1.astype(BF16), wu1.astype(BF16), wd1.astype(BF16),
      row(mix_norm), w_in_b, rope)

    assert bsz == SUBLANES
    are, aim, bre, bim, cre, cim = _ssm_params(ssm_A_re, ssm_A_im, ssm_log_dt,
                                               ssm_B_re, ssm_B_im, ssm_C_re, ssm_C_im, n_chain)
    n_state = are.shape[-1]
    tb_spec = pl.BlockSpec((tt, bsz * d_ssm), lambda i: (i, 0))
    y_ssm_tb = pl.pallas_call(
        functools.partial(_ssm_kernel, tt=tt, n_chain=n_chain),
        grid=(seq // tt,),
        in_specs=[tb_spec, _const_spec(are.shape), _const_spec(aim.shape),
                  _const_spec(bre.shape), _const_spec(bim.shape),
                  _const_spec(cre.shape), _const_spec(cim.shape),
                  _const_spec((1, d_ssm)), _const_spec((d_ssm, d_ssm)),
                  _const_spec((1, d_ssm)), _const_spec((1, d_ssm))],
        out_specs=tb_spec,
        out_shape=jax.ShapeDtypeStruct((seq, bsz * d_ssm), F32),
        scratch_shapes=([pltpu.VMEM((bsz, n_state), F32)] * 2
                        + [pltpu.VMEM((d_ssm // LANES, tt * bsz, LANES), F32)]
                        + [pltpu.VMEM((tt * bsz, n_state // n_chain), F32)] * (2 * n_chain)),
        compiler_params=cparams(("arbitrary",)),
        name="ssm",
    )(u_tb, are, aim, bre, bim, cre, cim,
      row(ssm_D), ssm_w_glu.astype(BF16), row(ssm_b_glu), row(ssm_out_norm))

    tq = nq * WINDOW
    cur = lambda j: pl.BlockSpec((None, tq, d_attn), lambda b, n: (b, n, j))
    prev = lambda j: pl.BlockSpec((None, WINDOW, d_attn),
                                  lambda b, n: (b, jnp.maximum(n * nq - 1, 0), j))
    y_attn = pl.pallas_call(
        functools.partial(_attn_kernel, nq=nq),
        grid=(bsz, seq // tq),
        in_specs=[pl.BlockSpec(memory_space=pltpu.SMEM),
                  cur(0), cur(1), prev(1), cur(2), prev(2),
                  pl.BlockSpec((1, d_attn), lambda b, n: (0, 0))],
        out_specs=cur(0),
        out_shape=jax.ShapeDtypeStruct((bsz, seq, d_attn), BF16),
        compiler_params=cparams(("parallel", "parallel")),
        name="attn",
    )(attn_sinks.astype(F32), qkv, qkv, qkv, qkv, qkv, row(attn_out_norm))

    out = pl.pallas_call(
        functools.partial(_out_ffn_kernel, fc=fc),
        grid=(bsz, seq // tm),
        in_specs=[tile(d_model),
                  tm_tile,
                  tile(d_attn), _const_spec((d_ssm + d_attn, d_model)),
                  _const_spec((1, d_model)), _const_spec((d_model, d_ff)),
                  _const_spec((d_model, d_ff)), _const_spec((d_ff, d_model)),
                  _const_spec((1, d_model))],
        out_specs=tile(d_model),
        out_shape=jax.ShapeDtypeStruct((bsz, seq, d_model), F32),
        compiler_params=cparams(("parallel", "parallel")),
        name="out_ffn",
    )(x1, y_ssm_tb, y_attn, w_out.astype(BF16),
      row(ffn2_norm), wg2.astype(BF16), wu2.astype(BF16), wd2.astype(BF16), row(final_norm))
    return out


def kernel(x, ffn1_norm, ffn1_w_gate, ffn1_w_up, ffn1_w_down, mix_norm, w_in, ssm_A_re, ssm_A_im, ssm_log_dt, ssm_B_re, ssm_B_im, ssm_C_re, ssm_C_im, ssm_D, ssm_w_glu, ssm_b_glu, attn_sinks, ssm_out_norm, attn_out_norm, w_out, ffn2_norm, ffn2_w_gate, ffn2_w_up, ffn2_w_down, final_norm):
    depth = ffn1_norm.shape[0]
    assert depth == 1, "the final norm is fused into the last layer's kernel"
    cs = _rope_lane_tables(x.shape[1])
    l = 0
    return _layer(x, cs, ffn1_norm[l], ffn1_w_gate[l], ffn1_w_up[l], ffn1_w_down[l], mix_norm[l],
                  w_in[l], ssm_A_re[l], ssm_A_im[l], ssm_log_dt[l], ssm_B_re[l], ssm_B_im[l],
                  ssm_C_re[l], ssm_C_im[l], ssm_D[l], ssm_w_glu[l], ssm_b_glu[l], attn_sinks[l],
                  ssm_out_norm[l], attn_out_norm[l], w_out[l], ffn2_norm[l], ffn2_w_gate[l],
                  ffn2_w_up[l], ffn2_w_down[l], final_norm)
```

```python
import functools
import math

import jax
import jax.numpy as jnp
from jax import lax
from jax.experimental import pallas as pl
from jax.experimental.pallas import tpu as pltpu

F32 = jnp.float32
BF16 = jnp.bfloat16

SSM_GROUP = 16
SSM_STATE = 64
HEAD_DIM = 64
N_KV_HEADS = 2
WINDOW = 128
ROPE_DIM = HEAD_DIM // 4
ROPE_THETA = 500000.0
RES_HALF = 0.5
EPS = 1e-6
NEG_INF = -1e30
LOG2E = math.log2(math.e)
LANES = 128
SUBLANES = 8
MXU_N = 256

VMEM_LIMIT = 56 * 1024 * 1024


def _rms(x, g):
    return x * lax.rsqrt(jnp.mean(x * x, axis=-1, keepdims=True) + EPS) * g


def _swiglu_residual(x, g_ref, wg_ref, wu_ref, wd_ref, fc):
    hn = _rms(x, g_ref[...]).astype(BF16)
    acc = x
    d_ff = wd_ref.shape[0]
    chunks = [slice(lo, min(lo + fc, d_ff)) for lo in range(0, d_ff, fc)]

    def gate_up(sl):
        return (jnp.dot(hn, wg_ref[:, sl], preferred_element_type=F32),
                jnp.dot(hn, wu_ref[:, sl], preferred_element_type=F32))

    nxt = gate_up(chunks[0])
    for i, sl in enumerate(chunks):
        g, u = nxt
        if i + 1 < len(chunks):
            nxt = gate_up(chunks[i + 1])
        a = (g * jax.nn.sigmoid(g) * u).astype(BF16)
        acc = acc + RES_HALF * jnp.dot(a, wd_ref[sl, :], preferred_element_type=F32)
    return acc


def _rope(t, c, s1, s2):
    half = ROPE_DIM // 2
    return t * c + pltpu.roll(t, half, 1) * s1 + pltpu.roll(t, LANES - half, 1) * s2


def _placed_kv(t):
    lane = lax.broadcasted_iota(jnp.int32, (1, LANES), 1)
    head0 = jnp.where(lane < HEAD_DIM, t, 0.0)
    head1 = jnp.where(lane < HEAD_DIM, 0.0, t)
    blocks = [head0, pltpu.roll(head0, HEAD_DIM, 1), pltpu.roll(head1, HEAD_DIM, 1), head1]
    return jnp.concatenate(blocks, axis=-1).astype(BF16)


def _ffn_in_kernel(x_ref, g1_ref, wg_ref, wu_ref, wd_ref, gm_ref, win_ref,
                   rc_ref, rs1_ref, rs2_ref,
                   x1_ref, u_ref, q_ref, k_ref, v_ref, *, fc, d_ssm, d_attn, d_kv):
    x1 = _swiglu_residual(x_ref[...], g1_ref, wg_ref, wu_ref, wd_ref, fc)
    x1_ref[...] = x1
    hn = _rms(x1, gm_ref[...]).astype(BF16)
    c, s1, s2 = rc_ref[...], rs1_ref[...], rs2_ref[...]
    half = x_ref.shape[0] // 2
    for rs in (slice(0, half), slice(half, 2 * half)):
        kv = jnp.dot(hn[rs], win_ref[:, d_ssm + d_attn:], preferred_element_type=F32)
        k_ref[rs, :] = _placed_kv(_rope(kv[:, :d_kv], c[rs], s1[rs], s2[rs]))
        v_ref[rs, :] = _placed_kv(kv[:, d_kv:])
    q = jnp.dot(hn, win_ref[:, d_ssm:d_ssm + d_attn], preferred_element_type=F32)
    for j in range(d_attn // LANES):
        sl = slice(j * LANES, (j + 1) * LANES)
        q_ref[:, sl] = _rope(q[:, sl], c, s1, s2).astype(BF16)
    u_ref[...] = jnp.dot(hn, win_ref[:, :d_ssm], preferred_element_type=F32)


def _ssm_kernel(u_ref, are_ref, aim_ref, bre_ref, bim_ref, cre_ref, cim_ref,
                d_ref, wglu_ref, bglu_ref, gn_ref,
                y_ref, xre_ref, xim_ref, tb_ref, *s_refs, tt, n_chain):
    sre, sim = s_refs[:n_chain], s_refs[n_chain:]
    rows = tt * SUBLANES
    d_ssm = d_ref.shape[-1]
    n_lane_blk = d_ssm // LANES
    cw = d_ssm // n_chain
    sw = are_ref.shape[-1] // n_chain

    @pl.when(pl.program_id(0) == 0)
    def _():
        xre_ref[...] = jnp.zeros_like(xre_ref)
        xim_ref[...] = jnp.zeros_like(xim_ref)

    for b in range(SUBLANES):
        for k in range(n_lane_blk):
            lo = b * d_ssm + k * LANES
            tb_ref[k, pl.ds(b, tt, stride=SUBLANES), :] = u_ref[:, lo:lo + LANES]
    u = jnp.concatenate([tb_ref[k] for k in range(n_lane_blk)], axis=-1)
    ub = u.astype(BF16)
    for c in range(n_chain):
        uc = ub[:, c * cw:(c + 1) * cw]
        sre[c][...] = jnp.dot(uc, bre_ref[c], preferred_element_type=F32)
        sim[c][...] = jnp.dot(uc, bim_ref[c], preferred_element_type=F32)

    ys = []
    for c in range(n_chain):
        st = slice(c * sw, (c + 1) * sw)
        ar, ai = are_ref[:, st], aim_ref[:, st]
        xr, xi = xre_ref[:, st], xim_ref[:, st]
        for t in range(tt):
            r = slice(t * SUBLANES, (t + 1) * SUBLANES)
            xr, xi = (ar * xr - ai * xi + sre[c][r, :],
                      ar * xi + ai * xr + sim[c][r, :])
            sre[c][r, :] = xr
            sim[c][r, :] = xi
        xre_ref[:, st] = xr
        xim_ref[:, st] = xi
        ys.append(jnp.dot(sre[c][...].astype(BF16), cre_ref[c], preferred_element_type=F32)
                  + jnp.dot(sim[c][...].astype(BF16), cim_ref[c], preferred_element_type=F32))
    y = jnp.concatenate(ys, axis=-1) + d_ref[...] * u
    y = jax.nn.gelu(y)
    z = jnp.dot(y.astype(BF16), wglu_ref[...], preferred_element_type=F32) + bglu_ref[...]
    y = y * jax.nn.sigmoid(z)
    y = _rms(y, gn_ref[...])
    for k in range(n_lane_blk):
        tb_ref[k] = y[:, k * LANES:(k + 1) * LANES]
    for b in range(SUBLANES):
        for k in range(n_lane_blk):
            lo = b * d_ssm + k * LANES
            y_ref[:, lo:lo + LANES] = tb_ref[k, pl.ds(b, tt, stride=SUBLANES), :]


def _attn_kernel(sink_ref, q_ref, kc_ref, kp_ref, vc_ref, vp_ref, gn_ref, o_ref, *, nq):
    blk = WINDOW
    n = pl.program_id(1)
    lane = lax.broadcasted_iota(jnp.int32, (1, LANES), 1)
    k_all = jnp.concatenate([kp_ref[...], kc_ref[...]], axis=0)
    v_all = jnp.concatenate([vp_ref[...], vc_ref[...]], axis=0)

    bf_tile = 2 * SUBLANES
    row0 = lax.broadcasted_iota(jnp.int32, (bf_tile, 1), 0) == 0
    kmat, vext, vhead = [], [], []
    for kh in range(N_KV_HEADS):
        kmat.append([]), vext.append([]), vhead.append([])
        for half in range(2):
            sl = slice((kh * 2 + half) * LANES, (kh * 2 + half + 1) * LANES)
            ones = jnp.broadcast_to(jnp.where((lane // HEAD_DIM) == half, 1.0, 0.0),
                                    (v_all.shape[0], LANES)).astype(BF16)
            kmat[kh].append(k_all[:, sl])
            vext[kh].append(jnp.concatenate([v_all[:, sl], ones], axis=-1))
            vhead[kh].append([
                jnp.concatenate(
                    [jnp.where(row0, 0.0, v_all[j * blk:j * blk + bf_tile, sl].astype(F32)).astype(BF16),
                     ones[:bf_tile]], axis=-1)
                for j in range(nq)])

    qi = lax.broadcasted_iota(jnp.int32, (blk, 2 * blk), 0)
    sj = lax.broadcasted_iota(jnp.int32, (blk, 2 * blk), 1)
    band = (sj > qi) & (sj <= qi + blk)
    band_first = band & ((sj >= blk) | (n > 0))
    col0 = sj == 0
    n_pairs = q_ref.shape[-1] // LANES
    heads_per_kv = (2 * n_pairs) // N_KV_HEADS

    outs = []
    for j in range(nq):
        valid = band_first if j == 0 else band
        pair_outs = []
        for pr in range(n_pairs):
            qp = q_ref[j * blk:(j + 1) * blk, pr * LANES:(pr + 1) * LANES]
            res = None
            for half in range(2):
                h = 2 * pr + half
                kh = h // heads_per_kv
                fill = jnp.where(col0, sink_ref[h] * LOG2E, NEG_INF)
                s = lax.dot_general(qp, kmat[kh][half][j * blk:(j + 2) * blk],
                                    (((1,), (1,)), ((), ())), preferred_element_type=F32)
                s = jnp.where(valid, s, fill)
                m = jnp.max(s, axis=-1, keepdims=True)
                p = jnp.exp2(s - m).astype(BF16)
                win = jnp.concatenate([vhead[kh][half][j],
                                       vext[kh][half][j * blk + bf_tile:(j + 2) * blk]], axis=0)
                o = jnp.dot(p, win, preferred_element_type=F32)
                res = o if res is None else res + o
            pair_outs.append(res[:, :LANES] / res[:, LANES:])
        outs.append(jnp.concatenate(pair_outs, axis=-1))
    y = jnp.concatenate(outs, axis=0)
    o_ref[...] = _rms(y, gn_ref[...]).astype(o_ref.dtype)


def _out_ffn_kernel(x1_ref, ys_ref, ya_ref, wo_ref, g2_ref, wg_ref, wu_ref, wd_ref, gf_ref,
                    o_ref, *, fc):
    d_ssm = ys_ref.shape[-1]
    x2 = (x1_ref[...]
          + jnp.dot(ys_ref[...].astype(BF16), wo_ref[:d_ssm, :], preferred_element_type=F32)
          + jnp.dot(ya_ref[...], wo_ref[d_ssm:, :], preferred_element_type=F32))
    x3 = _swiglu_residual(x2, g2_ref, wg_ref, wu_ref, wd_ref, fc)
    o_ref[...] = _rms(x3, gf_ref[...])


def _const_spec(shape):
    nd = len(shape)
    return pl.BlockSpec(shape, lambda *_: (0,) * nd, pipeline_mode=pl.Buffered(1))


def _rope_lane_tables(seq):
    half = ROPE_DIM // 2
    j = jnp.arange(LANES) % HEAD_DIM
    inv_freq = ROPE_THETA ** (-(j % half).astype(F32) * 2.0 / ROPE_DIM)
    ang = jnp.arange(seq, dtype=F32)[:, None] * inv_freq[None, :]
    cos, sin = jnp.cos(ang), jnp.sin(ang)
    c = jnp.where(j < ROPE_DIM, cos, 1.0)
    s1 = jnp.where((j >= half) & (j < ROPE_DIM), sin, 0.0)
    s2 = jnp.where(j < half, -sin, 0.0)
    return c, s1, s2


def _block_diag(m, n_halves):
    g, r, c = m.shape
    gh = g // n_halves
    m = m.reshape(n_halves, gh, r, c)
    eye = jnp.eye(gh, dtype=m.dtype)
    return jnp.einsum('hgrc,gk->hgrkc', m, eye).reshape(n_halves, gh * r, gh * c)


def _discretize_kernel(are_ref, aim_ref, ldt_ref, bre_ref, bim_ref,
                       lre_ref, lim_ref, bbre_ref, bbim_ref):
    a_re, a_im = are_ref[...], aim_ref[...]
    dt = jnp.exp(ldt_ref[...])
    mag = jnp.exp(a_re * dt)
    l_re = mag * jnp.cos(a_im * dt)
    l_im = mag * jnp.sin(a_im * dt)
    lre_ref[...] = l_re
    lim_ref[...] = l_im
    n_re = l_re - 1.0
    den = a_re * a_re + a_im * a_im
    z_re = ((n_re * a_re + l_im * a_im) / den)[:, None, :]
    z_im = ((l_im * a_re - n_re * a_im) / den)[:, None, :]
    b_re, b_im = bre_ref[...], bim_ref[...]
    bbre_ref[...] = z_re * b_re - z_im * b_im
    bbim_ref[...] = z_re * b_im + z_im * b_re


def _ssm_params(a_re, a_im, log_dt, b_re, b_im, c_re, c_im, n_chain):
    g, p, c = b_re.shape
    lam_re, lam_im, bb_re, bb_im = pl.pallas_call(
        _discretize_kernel,
        out_shape=[jax.ShapeDtypeStruct((g, p), F32)] * 2 + [jax.ShapeDtypeStruct((g, c, p), F32)] * 2,
        name="ssm_discretize",
    )(a_re, a_im, log_dt.reshape(g, 1), jnp.swapaxes(b_re, 1, 2), jnp.swapaxes(b_im, 1, 2))
    n_state = g * p
    are = jnp.broadcast_to(lam_re.reshape(1, n_state), (SUBLANES, n_state))
    aim = jnp.broadcast_to(lam_im.reshape(1, n_state), (SUBLANES, n_state))
    bre = _block_diag(bb_re, n_chain).astype(BF16)
    bim = _block_diag(bb_im, n_chain).astype(BF16)
    cre = _block_diag(jnp.swapaxes(c_re, 1, 2), n_chain).astype(BF16)
    cim = _block_diag(jnp.swapaxes(-c_im, 1, 2), n_chain).astype(BF16)
    return are, aim, bre, bim, cre, cim


def _layer(x, cs, ffn1_norm, wg1, wu1, wd1, mix_norm, w_in, ssm_A_re, ssm_A_im, ssm_log_dt,
           ssm_B_re, ssm_B_im, ssm_C_re, ssm_C_im, ssm_D, ssm_w_glu, ssm_b_glu, attn_sinks,
           ssm_out_norm, attn_out_norm, w_out, ffn2_norm, wg2, wu2, wd2, final_norm):
    bsz, seq, d_model = x.shape
    d_ff = wg1.shape[1]
    d_ssm = ssm_D.shape[0]
    d_attn = attn_out_norm.shape[0]
    d_kv = N_KV_HEADS * HEAD_DIM
    d_kvp = 2 * N_KV_HEADS * LANES
    assert d_kv == LANES
    d_in = w_in.shape[1]
    tm = 1024
    fc = MXU_N
    tt = 128
    n_chain = 4
    nq = 16
    row = lambda v: v.reshape(1, -1).astype(F32)

    col_scale = jnp.concatenate([jnp.ones((d_ssm,), F32),
                                 jnp.full((d_attn,), LOG2E / math.sqrt(HEAD_DIM), F32),
                                 jnp.ones((2 * d_kv,), F32)])
    w_in_b = (w_in * col_scale[None, :]).astype(BF16)

    cparams = lambda sem: pltpu.CompilerParams(dimension_semantics=sem, vmem_limit_bytes=VMEM_LIMIT)

    tile = lambda w: pl.BlockSpec((None, tm, w), lambda b, i: (b, i, 0))
    rope_spec = pl.BlockSpec((tm, LANES), lambda b, i: (i, 0))
    tm_tile = pl.BlockSpec((tm, d_ssm), lambda b, i: (i, b))
    x1, u_tb, q, k, v = pl.pallas_call(
        functools.partial(_ffn_in_kernel, fc=fc, d_ssm=d_ssm, d_attn=d_attn, d_kv=d_kv),
        grid=(bsz, seq // tm),
        in_specs=[tile(d_model), _const_spec((1, d_model)), _const_spec((d_model, d_ff)),
                  _const_spec((d_model, d_ff)), _const_spec((d_ff, d_model)),
                  _const_spec((1, d_model)), _const_spec((d_model, d_in)),
                  rope_spec, rope_spec, rope_spec],
        out_specs=[tile(d_model),
                   tm_tile,
                   tile(d_attn), tile(d_kvp), tile(d_kvp)],
        out_shape=[jax.ShapeDtypeStruct((bsz, seq, d_model), F32),
                   jax.ShapeDtypeStruct((seq, bsz * d_ssm), F32),
                   jax.ShapeDtypeStruct((bsz, seq, d_attn), BF16),
                   jax.ShapeDtypeStruct((bsz, seq, d_kvp), BF16),
                   jax.ShapeDtypeStruct((bsz, seq, d_kvp), BF16)],
        compiler_params=cparams(("parallel", "parallel")),
        name="ffn_in",
    )(x, row(ffn1_norm), wg1.astype(BF16), wu1.astype(BF16), wd1.astype(BF16),
      row(mix_norm), w_in_b, *cs)

    assert bsz == SUBLANES
    are, aim, bre, bim, cre, cim = _ssm_params(ssm_A_re, ssm_A_im, ssm_log_dt,
                                               ssm_B_re, ssm_B_im, ssm_C_re, ssm_C_im, n_chain)
    n_state = are.shape[-1]
    tb_spec = pl.BlockSpec((tt, bsz * d_ssm), lambda i: (i, 0))
    y_ssm_tb = pl.pallas_call(
        functools.partial(_ssm_kernel, tt=tt, n_chain=n_chain),
        grid=(seq // tt,),
        in_specs=[tb_spec, _const_spec(are.shape), _const_spec(aim.shape),
                  _const_spec(bre.shape), _const_spec(bim.shape),
                  _const_spec(cre.shape), _const_spec(cim.shape),
                  _const_spec((1, d_ssm)), _const_spec((d_ssm, d_ssm)),
                  _const_spec((1, d_ssm)), _const_spec((1, d_ssm))],
        out_specs=tb_spec,
        out_shape=jax.ShapeDtypeStruct((seq, bsz * d_ssm), F32),
        scratch_shapes=([pltpu.VMEM((bsz, n_state), F32)] * 2
                        + [pltpu.VMEM((d_ssm // LANES, tt * bsz, LANES), F32)]
                        + [pltpu.VMEM((tt * bsz, n_state // n_chain), F32)] * (2 * n_chain)),
        compiler_params=cparams(("arbitrary",)),
        name="ssm",
    )(u_tb, are, aim, bre, bim, cre, cim,
      row(ssm_D), ssm_w_glu.astype(BF16), row(ssm_b_glu), row(ssm_out_norm))

    tq = nq * WINDOW
    cur = lambda w: pl.BlockSpec((None, tq, w), lambda b, n: (b, n, 0))
    prev = lambda w: pl.BlockSpec((None, WINDOW, w), lambda b, n: (b, jnp.maximum(n * nq - 1, 0), 0))
    y_attn = pl.pallas_call(
        functools.partial(_attn_kernel, nq=nq),
        grid=(bsz, seq // tq),
        in_specs=[pl.BlockSpec(memory_space=pltpu.SMEM),
                  cur(d_attn), cur(d_kvp), prev(d_kvp), cur(d_kvp), prev(d_kvp),
                  pl.BlockSpec((1, d_attn), lambda b, n: (0, 0))],
        out_specs=cur(d_attn),
        out_shape=jax.ShapeDtypeStruct((bsz, seq, d_attn), BF16),
        compiler_params=cparams(("parallel", "parallel")),
        name="attn",
    )(attn_sinks.astype(F32), q, k, k, v, v, row(attn_out_norm))

    out = pl.pallas_call(
        functools.partial(_out_ffn_kernel, fc=fc),
        grid=(bsz, seq // tm),
        in_specs=[tile(d_model),
                  tm_tile,
                  tile(d_attn), _const_spec((d_ssm + d_attn, d_model)),
                  _const_spec((1, d_model)), _const_spec((d_model, d_ff)),
                  _const_spec((d_model, d_ff)), _const_spec((d_ff, d_model)),
                  _const_spec((1, d_model))],
        out_specs=tile(d_model),
        out_shape=jax.ShapeDtypeStruct((bsz, seq, d_model), F32),
        compiler_params=cparams(("parallel", "parallel")),
        name="out_ffn",
    )(x1, y_ssm_tb, y_attn, w_out.astype(BF16),
      row(ffn2_norm), wg2.astype(BF16), wu2.astype(BF16), wd2.astype(BF16), row(final_norm))
    return out


def kernel(x, ffn1_norm, ffn1_w_gate, ffn1_w_up, ffn1_w_down, mix_norm, w_in, ssm_A_re, ssm_A_im, ssm_log_dt, ssm_B_re, ssm_B_im, ssm_C_re, ssm_C_im, ssm_D, ssm_w_glu, ssm_b_glu, attn_sinks, ssm_out_norm, attn_out_norm, w_out, ffn2_norm, ffn2_w_gate, ffn2_w_up, ffn2_w_down, final_norm):
    depth = ffn1_norm.shape[0]
    assert depth == 1, "the final norm is fused into the last layer's kernel"
    cs = _rope_lane_tables(x.shape[1])
    l = 0
    return _layer(x, cs, ffn1_norm[l], ffn1_w_gate[l], ffn1_w_up[l], ffn1_w_down[l], mix_norm[l],
                  w_in[l], ssm_A_re[l], ssm_A_im[l], ssm_log_dt[l], ssm_B_re[l], ssm_B_im[l],
                  ssm_C_re[l], ssm_C_im[l], ssm_D[l], ssm_w_glu[l], ssm_b_glu[l], attn_sinks[l],
                  ssm_out_norm[l], attn_out_norm[l], w_out[l], ffn2_norm[l], ffn2_w_gate[l],
                  ffn2_w_up[l], ffn2_w_down[l], final_norm)
```
